```python
import jax, jax.numpy as jnp
from jax import lax
import numpy as np

D_MODEL = 1024
BATCH = 4
SEQ = 4096
DEPTH = 2

N_MIXERS = 2
N_A_LAYERS = (DEPTH + 1) // 2
N_B_LAYERS = DEPTH // 2
CHUNK = 128
A_WIDTH = D_MODEL
A_GROUPS = 16
A_GROUP_DIM = A_WIDTH // A_GROUPS
B_HEADS = 16
B_HEAD_DIM = D_MODEL // B_HEADS
B_PATTERNS = ((128, 1), (512, 4), (2048, 16))
N_PAT = len(B_PATTERNS)
D_FF = 4 * D_MODEL
ALPHA = (2 * DEPTH) ** 0.25
BETA = (8 * DEPTH) ** -0.25
LN_EPS = 1e-5
ADA_SCALE = 0.1
NEG = -1e30

kernel_name = "hybrid_gmlp_dilated_attn_deepnorm_adaln"


def layer_norm(x, g, b):
    xf = x.astype(jnp.float32)
    mu = jnp.mean(xf, axis=-1, keepdims=True)
    var = jnp.mean(jnp.square(xf - mu), axis=-1, keepdims=True)
    y = (xf - mu) * lax.rsqrt(var + LN_EPS) * g.astype(jnp.float32) + b.astype(jnp.float32)
    return y.astype(x.dtype)


def ada_mod(c, w, b):
    m = jax.nn.silu(c) @ w + b
    shift, scale, gate = jnp.split(m, 3, axis=-1)
    return shift[:, None, :], scale[:, None, :], 1.0 + gate[:, None, :]


def alibi_slopes(n_heads):
    h = jnp.arange(1, n_heads + 1, dtype=jnp.float32)
    return jnp.exp2(-8.0 * h / n_heads)


def mixer_a(h, w_in, b_in, vn_g, vn_b, w_s, b_s, w_out):
    B, S, _ = h.shape
    uv = jax.nn.gelu(h @ w_in + b_in)
    u, v = jnp.split(uv, 2, axis=-1)
    v = layer_norm(v, vn_g, vn_b)
    v = v.reshape(B, S // CHUNK, CHUNK, A_GROUPS, A_GROUP_DIM)
    causal = jnp.tril(jnp.ones((CHUNK, CHUNK), dtype=bool))
    w_causal = jnp.where(causal, w_s, 0.0)
    z = jnp.einsum('gts,bnsgc->bntgc', w_causal, v) + b_s.T[:, :, None]
    z = z.reshape(B, S, A_WIDTH)
    return (u * z) @ w_out


def dilated_branch(q, k, v, window, dilation, slopes):
    B, S, H, E = q.shape
    span = window // dilation
    seg = span * dilation
    S_pad = -(-S // seg) * seg
    nb = S_pad // seg
    pad = ((0, 0), (0, S_pad - S), (0, 0), (0, 0))

    def to_blocks(t):
        return jnp.pad(t, pad).reshape(B, nb, span, dilation, H, E)

    def with_prev(t):
        prev = jnp.concatenate([jnp.zeros_like(t[:, :1]), t[:, :-1]], axis=1)
        return jnp.concatenate([prev, t], axis=2)

    qb = to_blocks(q)
    kb = with_prev(to_blocks(k))
    vb = with_prev(to_blocks(v))
    s = jnp.einsum('bnqrhe,bnkrhe->bnrhqk', qb, kb,
                   preferred_element_type=jnp.float32) * (E ** -0.5)
    qi = jnp.arange(span)[:, None]
    ki = jnp.arange(2 * span)[None, :]
    diff = span + qi - ki
    blk = jnp.arange(nb)[:, None, None]
    valid = (diff >= 0) & (diff <= span) & (blk * span + ki - span >= 0)
    bias = -slopes[:, None, None] * (dilation * diff).astype(jnp.float32)
    s = s + bias
    s = jnp.where(valid[None, :, None, None], s, NEG)
    m = jnp.max(s, axis=-1, keepdims=True)
    p = jnp.exp(s - m)
    l = jnp.sum(p, axis=-1, keepdims=True)
    o = jnp.einsum('bnrhqk,bnkrhe->bnqrhe', p / l, vb.astype(jnp.float32))
    lse = (m + jnp.log(l))[..., 0].transpose(0, 1, 4, 2, 3)
    o = o.reshape(B, S_pad, H, E)[:, :S]
    lse = lse.reshape(B, S_pad, H)[:, :S]
    return o, lse


def mixer_b(h, w_qkv, w_out, slopes):
    B, S, _ = h.shape
    qkv = (h @ w_qkv).reshape(B, S, N_PAT, 3, B_HEADS, B_HEAD_DIM)
    outs, lses = [], []
    for g, (window, dilation) in enumerate(B_PATTERNS):
        o, lse = dilated_branch(qkv[:, :, g, 0], qkv[:, :, g, 1], qkv[:, :, g, 2],
                                window, dilation, slopes)
        outs.append(o)
        lses.append(lse)
    wts = jax.nn.softmax(jnp.stack(lses, axis=0), axis=0)[..., None]
    o = jnp.sum(wts * jnp.stack(outs, axis=0), axis=0)
    return o.reshape(B, S, B_HEADS * B_HEAD_DIM).astype(h.dtype) @ w_out


def squared_relu_mlp(h, w_up, w_down):
    return jnp.square(jax.nn.relu(h @ w_up)) @ w_down


def setup_inputs(seed: int = 0) -> dict:
    key = jax.random.key(seed)
    ks = jax.random.split(key, 18)
    f32 = jnp.float32
    nrm = lambda k, shape: jax.random.normal(k, shape, dtype=f32)
    return {
        "x": nrm(ks[0], (BATCH, SEQ, D_MODEL)),
        "c": nrm(ks[1], (BATCH, D_MODEL)),
        "ada_w": nrm(ks[2], (DEPTH, 2, D_MODEL, 3 * D_MODEL)) * (D_MODEL ** -0.5) * ADA_SCALE,
        "ada_b": nrm(ks[3], (DEPTH, 2, 3 * D_MODEL)) * 0.01,
        "ln_g": 1.0 + 0.02 * nrm(ks[4], (DEPTH, 2, D_MODEL)),
        "ln_b": 0.02 * nrm(ks[5], (DEPTH, 2, D_MODEL)),
        "a_w_in": nrm(ks[6], (N_A_LAYERS, D_MODEL, 2 * A_WIDTH)) * (D_MODEL ** -0.5),
        "a_b_in": 0.02 * nrm(ks[7], (N_A_LAYERS, 2 * A_WIDTH)),
        "a_vn_g": 1.0 + 0.02 * nrm(ks[8], (N_A_LAYERS, A_WIDTH)),
        "a_vn_b": 0.02 * nrm(ks[9], (N_A_LAYERS, A_WIDTH)),
        "a_w_s": nrm(ks[10], (N_A_LAYERS, A_GROUPS, CHUNK, CHUNK)) * (CHUNK ** -0.5),
        "a_b_s": 1.0 + 0.02 * nrm(ks[11], (N_A_LAYERS, A_GROUPS, CHUNK)),
        "a_w_out": nrm(ks[12], (N_A_LAYERS, A_WIDTH, D_MODEL)) * (A_WIDTH ** -0.5) * BETA,
        "b_w_qkv": nrm(ks[13], (N_B_LAYERS, D_MODEL, N_PAT * 3 * B_HEADS * B_HEAD_DIM)) * (D_MODEL ** -0.5),
        "b_w_out": nrm(ks[14], (N_B_LAYERS, B_HEADS * B_HEAD_DIM, D_MODEL)) * ((B_HEADS * B_HEAD_DIM) ** -0.5) * BETA,
        "mlp_w_up": nrm(ks[15], (DEPTH, D_MODEL, D_FF)) * (D_MODEL ** -0.5),
        "mlp_w_down": nrm(ks[16], (DEPTH, D_FF, D_MODEL)) * (D_FF ** -0.5) * BETA,
    }


def reference(x, c, ada_w, ada_b, ln_g, ln_b, a_w_in, a_b_in, a_vn_g, a_vn_b, a_w_s, a_b_s,
              a_w_out, b_w_qkv, b_w_out, mlp_w_up, mlp_w_down):
    slopes = alibi_slopes(B_HEADS)
    for i in range(DEPTH):
        j = i // N_MIXERS
        shift, scale, gate = ada_mod(c, ada_w[i, 0], ada_b[i, 0])
        h = x * (1.0 + scale) + shift
        if i % N_MIXERS == 0:
            y = mixer_a(h, a_w_in[j], a_b_in[j], a_vn_g[j], a_vn_b[j], a_w_s[j], a_b_s[j], a_w_out[j])
        else:
            y = mixer_b(h, b_w_qkv[j], b_w_out[j], slopes)
        x = layer_norm(ALPHA * x + gate * y, ln_g[i, 0], ln_b[i, 0])
        shift, scale, gate = ada_mod(c, ada_w[i, 1], ada_b[i, 1])
        h = x * (1.0 + scale) + shift
        y = squared_relu_mlp(h, mlp_w_up[i], mlp_w_down[i])
        x = layer_norm(ALPHA * x + gate * y, ln_g[i, 1], ln_b[i, 1])
    return x
```

```python
import jax
import jax.numpy as jnp
from jax import lax
from jax.experimental import pallas as pl
from jax.experimental.pallas import tpu as pltpu

D_MODEL = 1024
DEPTH = 2
CHUNK = 128
A_GROUPS = 16
A_GROUP_DIM = D_MODEL // A_GROUPS
B_HEADS = 16
B_HEAD_DIM = D_MODEL // B_HEADS
B_PATTERNS = ((128, 1), (512, 4), (2048, 16))
N_PAT = len(B_PATTERNS)
SPAN = 128
D_FF = 4 * D_MODEL
ALPHA = (2 * DEPTH) ** 0.25
LN_EPS = 1e-5
NEG = -1e30

LANES = 128
VMEM_LIMIT_BYTES = 56 * 1024 * 1024

F32 = jnp.float32
BF16 = jnp.bfloat16


def _layer_norm(x, g, b):
    mu = jnp.mean(x, axis=-1, keepdims=True)
    xc = x - mu
    var = jnp.mean(xc * xc, axis=-1, keepdims=True)
    return xc * lax.rsqrt(var + LN_EPS) * g + b


def _resident(shape):
    zeros = (0,) * len(shape)
    return pl.BlockSpec(shape, lambda *_: zeros, pipeline_mode=pl.Buffered(1))


def _ada_kernel(c_ref, w_ref, b_ref, o_ref):
    s = jax.nn.silu(c_ref[...])
    o_ref[0] = jnp.dot(s, w_ref[0], preferred_element_type=F32,
                       precision=lax.Precision.HIGHEST) + b_ref[0]


def _ada_mod(c, ada_w, ada_b):
    batch = c.shape[0]
    rows = 8
    c_pad = jnp.zeros((rows, D_MODEL), F32).at[:batch].set(c)
    n_sub = DEPTH * 2
    w = ada_w.reshape(n_sub, D_MODEL, 3 * D_MODEL)
    b = ada_b.reshape(n_sub, 1, 3 * D_MODEL)
    tn = 1024
    out = pl.pallas_call(
        _ada_kernel,
        grid=(n_sub, 3 * D_MODEL // tn),
        in_specs=[
            pl.BlockSpec((rows, D_MODEL), lambda i, j: (0, 0)),
            pl.BlockSpec((1, D_MODEL, tn), lambda i, j: (i, 0, j)),
            pl.BlockSpec((1, 1, tn), lambda i, j: (i, 0, j)),
        ],
        out_specs=pl.BlockSpec((1, rows, tn), lambda i, j: (i, 0, j)),
        out_shape=jax.ShapeDtypeStruct((n_sub, rows, 3 * D_MODEL), F32),
        name="ada_mod",
    )(c_pad, w, b)
    return out[:, :batch].reshape(n_sub, batch, 3, D_MODEL)


def _modulate(x, mod):
    shift = mod[0:1]
    scale = mod[1:2]
    return (x * (1.0 + scale) + shift).astype(BF16)


def _post_norm(x, y, mod, g, b):
    gate = 1.0 + mod[2:3]
    return _layer_norm(ALPHA * x + gate * y, g, b)


def _mixer_a_kernel(x_ref, mod_ref, w_in_ref, b_in_ref, vn_g_ref, vn_b_ref, ws_ref, bs_ref,
                    w_out_ref, ln_g_ref, ln_b_ref, o_ref, ws_scr, uv_scr, v_scr, g_scr):
    tm = x_ref.shape[0]

    @pl.when(jnp.logical_and(pl.program_id(0) == 0, pl.program_id(1) == 0))
    def _():
        t = lax.broadcasted_iota(jnp.int32, (CHUNK, CHUNK), 0)
        s = lax.broadcasted_iota(jnp.int32, (CHUNK, CHUNK), 1)
        causal = s <= t
        for g in range(A_GROUPS):
            ws_scr[g] = jnp.where(causal, ws_ref[g], 0.0).astype(BF16)

    x = x_ref[...]
    mod = mod_ref[...]
    h = _modulate(x, mod)
    uv = jnp.dot(h, w_in_ref[...], preferred_element_type=F32) + b_in_ref[...]
    uv_scr[...] = jax.nn.gelu(uv)
    v = _layer_norm(uv_scr[:, D_MODEL:], vn_g_ref[...], vn_b_ref[...])
    v_scr[...] = v.astype(BF16)

    low_half = lax.broadcasted_iota(jnp.int32, (CHUNK, LANES), 1) < A_GROUP_DIM

    def chunk_body(c, carry):
        r0 = pl.multiple_of(c * CHUNK, CHUNK)
        for gp in range(A_GROUPS // 2):
            cols = slice(gp * LANES, (gp + 1) * LANES)
            vb = v_scr[pl.ds(r0, CHUNK), cols]
            za = jnp.dot(ws_scr[2 * gp], vb, preferred_element_type=F32)
            zb = jnp.dot(ws_scr[2 * gp + 1], vb, preferred_element_type=F32)
            z = jnp.where(low_half, za, zb) + bs_ref[:, cols]
            u = uv_scr[pl.ds(r0, CHUNK), cols]
            g_scr[pl.ds(r0, CHUNK), cols] = (u * z).astype(BF16)
        return carry

    lax.fori_loop(0, tm // CHUNK, chunk_body, 0)

    y = jnp.dot(g_scr[...], w_out_ref[...], preferred_element_type=F32)
    o_ref[...] = _post_norm(x, y, mod, ln_g_ref[...], ln_b_ref[...])


def _mixer_a(x, mod, w_in, b_in, vn_g, vn_b, w_s, b_s, w_out, ln_g, ln_b):
    batch, seq, _ = x.shape
    tm = 512
    row = lambda a: a.reshape(1, -1)
    bs_plane = jnp.repeat(b_s.T, A_GROUP_DIM, axis=1)
    return pl.pallas_call(
        _mixer_a_kernel,
        grid=(batch, seq // tm),
        in_specs=[
            pl.BlockSpec((None, tm, D_MODEL), lambda b, i: (b, i, 0)),
            pl.BlockSpec((None, 3, D_MODEL), lambda b, i: (b, 0, 0)),
            _resident((D_MODEL, 2 * D_MODEL)),
            _resident((1, 2 * D_MODEL)),
            _resident((1, D_MODEL)),
            _resident((1, D_MODEL)),
            _resident((A_GROUPS, CHUNK, CHUNK)),
            _resident((CHUNK, D_MODEL)),
            _resident((D_MODEL, D_MODEL)),
            _resident((1, D_MODEL)),
            _resident((1, D_MODEL)),
        ],
        out_specs=pl.BlockSpec((None, tm, D_MODEL), lambda b, i: (b, i, 0)),
        out_shape=jax.ShapeDtypeStruct(x.shape, F32),
        scratch_shapes=[
            pltpu.VMEM((A_GROUPS, CHUNK, CHUNK), BF16),
            pltpu.VMEM((tm, 2 * D_MODEL), F32),
            pltpu.VMEM((tm, D_MODEL), BF16),
            pltpu.VMEM((tm, D_MODEL), BF16),
        ],
        compiler_params=pltpu.CompilerParams(
            dimension_semantics=("arbitrary", "arbitrary"),
            vmem_limit_bytes=VMEM_LIMIT_BYTES),
        name="mixer_a",
    )(x, mod, w_in.astype(BF16), row(b_in), row(vn_g), row(vn_b), w_s, bs_plane,
      w_out.astype(BF16), row(ln_g), row(ln_b))


def _mlp_kernel(x_ref, mod_ref, w_up_ref, w_down_ref, ln_g_ref, ln_b_ref, o_ref):
    x = x_ref[...]
    mod = mod_ref[...]
    h = _modulate(x, mod)
    a = jnp.dot(h, w_up_ref[...], preferred_element_type=F32)
    r = jnp.square(jnp.maximum(a, 0.0)).astype(BF16)
    y = jnp.dot(r, w_down_ref[...], preferred_element_type=F32)
    o_ref[...] = _post_norm(x, y, mod, ln_g_ref[...], ln_b_ref[...])


def _mlp(x, mod, w_up, w_down, ln_g, ln_b):
    batch, seq, _ = x.shape
    tm = 512
    row = lambda a: a.reshape(1, -1)
    return pl.pallas_call(
        _mlp_kernel,
        grid=(batch, seq // tm),
        in_specs=[
            pl.BlockSpec((None, tm, D_MODEL), lambda b, i: (b, i, 0)),
            pl.BlockSpec((None, 3, D_MODEL), lambda b, i: (b, 0, 0)),
            _resident((D_MODEL, D_FF)),
            _resident((D_FF, D_MODEL)),
            _resident((1, D_MODEL)),
            _resident((1, D_MODEL)),
        ],
        out_specs=pl.BlockSpec((None, tm, D_MODEL), lambda b, i: (b, i, 0)),
        out_shape=jax.ShapeDtypeStruct(x.shape, F32),
        compiler_params=pltpu.CompilerParams(
            dimension_semantics=("arbitrary", "arbitrary"),
            vmem_limit_bytes=VMEM_LIMIT_BYTES),
        name="mlp",
    )(x, mod, w_up.astype(BF16), w_down.astype(BF16), row(ln_g), row(ln_b))


def _qkv_kernel(x_ref, mod_ref, w_ref, o_ref, h_scr):
    @pl.when(pl.program_id(2) == 0)
    def _():
        h_scr[...] = _modulate(x_ref[...], mod_ref[...])

    o_ref[...] = jnp.dot(h_scr[...], w_ref[...], preferred_element_type=F32).astype(BF16)


def _qkv(x, mod, w_qkv):
    batch, seq, _ = x.shape
    n_out = w_qkv.shape[1]
    tm, tn = 1024, 1024
    return pl.pallas_call(
        _qkv_kernel,
        grid=(batch, seq // tm, n_out // tn),
        in_specs=[
            pl.BlockSpec((None, tm, D_MODEL), lambda b, i, j: (b, i, 0)),
            pl.BlockSpec((None, 3, D_MODEL), lambda b, i, j: (b, 0, 0)),
            pl.BlockSpec((D_MODEL, tn), lambda b, i, j: (0, j)),
        ],
        out_specs=pl.BlockSpec((None, tm, tn), lambda b, i, j: (b, i, j)),
        out_shape=jax.ShapeDtypeStruct((batch, seq, n_out), BF16),
        scratch_shapes=[pltpu.VMEM((tm, D_MODEL), BF16)],
        compiler_params=pltpu.CompilerParams(
            dimension_semantics=("arbitrary", "arbitrary", "arbitrary"),
            vmem_limit_bytes=VMEM_LIMIT_BYTES),
        name="qkv_proj",
    )(x, mod, w_qkv.astype(BF16))


def _attn_kernel(*refs):
    n_in = 5 * N_PAT
    in_refs = refs[:n_in]
    out_refs = refs[n_in:n_in + 2 * N_PAT]
    bias_scr = refs[n_in + 2 * N_PAT]
    unit = pl.program_id(1)

    @pl.when(jnp.logical_and(pl.program_id(0) == 0, unit == 0))
    def _():
        qi = lax.broadcasted_iota(jnp.int32, (SPAN, 2 * SPAN), 0)
        ki = lax.broadcasted_iota(jnp.int32, (SPAN, 2 * SPAN), 1)
        diff = SPAN + qi - ki
        valid = jnp.logical_and(diff >= 0, diff <= SPAN)
        for g, (_, dilation) in enumerate(B_PATTERNS):
            dist = (dilation * diff).astype(F32)
            for hd in range(B_HEADS):
                slope = 2.0 ** (-8.0 * (hd + 1) / B_HEADS)
                bias_scr[g, hd] = jnp.where(valid, -slope * dist, NEG)

    lane = lax.broadcasted_iota(jnp.int32, (SPAN, LANES), 1)
    low_half = lane < B_HEAD_DIM
    scale = B_HEAD_DIM ** -0.5
    q_mask = (jnp.where(low_half, scale, 0.0).astype(BF16),
              jnp.where(low_half, 0.0, scale).astype(BF16))

    for g, (_, dilation) in enumerate(B_PATTERNS):
        q_ref, kc_ref, kp_ref, vc_ref, vp_ref = in_refs[5 * g:5 * g + 5]
        o_ref, lse_ref = out_refs[2 * g:2 * g + 2]
        prev_pen = jnp.where(unit // dilation == 0, NEG, 0.0).astype(F32)

        def pair_body(hp, lse_tile, g=g, q_ref=q_ref, kc_ref=kc_ref, kp_ref=kp_ref,
                      vc_ref=vc_ref, vp_ref=vp_ref, o_ref=o_ref, prev_pen=prev_pen):
            cols = pl.ds(pl.multiple_of(hp * LANES, LANES), LANES)
            q2 = q_ref[:, cols]
            k2 = jnp.concatenate([kp_ref[:, cols], kc_ref[:, cols]], axis=0)
            v2 = jnp.concatenate([vp_ref[:, cols], vc_ref[:, cols]], axis=0)
            outs = []
            for j in range(2):
                hd = 2 * hp + j
                s = lax.dot_general(q2 * q_mask[j], k2, (((1,), (1,)), ((), ())),
                                    preferred_element_type=F32)
                s = s + bias_scr[g, hd]
                s = jnp.concatenate([s[:, :SPAN] + prev_pen, s[:, SPAN:]], axis=1)
                m = jnp.max(s, axis=-1, keepdims=True)
                p = jnp.exp(s - m)
                l = jnp.sum(p, axis=-1, keepdims=True)
                pv = jnp.dot(p.astype(BF16), v2, preferred_element_type=F32)
                outs.append(pv / l)
                lse_tile = jnp.where(lane == hd, m + jnp.log(l), lse_tile)
            o_ref[:, cols] = jnp.where(low_half, outs[0], outs[1]).astype(BF16)
            return lse_tile

        lse_ref[...] = lax.fori_loop(0, B_HEADS // 2, pair_body,
                                     jnp.zeros((SPAN, LANES), F32))


def _attention(qkv):
    batch, seq, n_qkv = qkv.shape
    width = B_HEADS * B_HEAD_DIM
    n_units = seq // SPAN
    col_blocks = n_qkv // width

    in_specs, operands, out_specs, out_shapes = [], [], [], []
    for g, (_, d) in enumerate(B_PATTERNS):
        view = qkv.reshape(batch, seq // d, d * n_qkv)

        def spec(part, prev, g=g, d=d):
            def index_map(b, u):
                blk = u // d
                if prev:
                    blk = jnp.maximum(blk - 1, 0)
                return (b, blk, (u % d) * col_blocks + g * 3 + part)
            return pl.BlockSpec((None, SPAN, width), index_map)

        in_specs += [spec(0, False), spec(1, False), spec(1, True), spec(2, False), spec(2, True)]
        operands += [view] * 5
        out_map = lambda b, u, d=d: (b, u // d, u % d)
        out_specs += [pl.BlockSpec((None, SPAN, width), out_map),
                      pl.BlockSpec((None, SPAN, LANES), out_map)]
        out_shapes += [jax.ShapeDtypeStruct((batch, seq // d, d * width), BF16),
                       jax.ShapeDtypeStruct((batch, seq // d, d * LANES), F32)]

    outs = pl.pallas_call(
        _attn_kernel,
        grid=(batch, n_units),
        in_specs=in_specs,
        out_specs=out_specs,
        out_shape=out_shapes,
        scratch_shapes=[pltpu.VMEM((N_PAT, B_HEADS, SPAN, 2 * SPAN), F32)],
        compiler_params=pltpu.CompilerParams(
            dimension_semantics=("arbitrary", "arbitrary"),
            vmem_limit_bytes=VMEM_LIMIT_BYTES),
        name="dilated_attn",
    )(*operands)
    o = [outs[2 * g].reshape(batch, seq, width) for g in range(N_PAT)]
    lse = [outs[2 * g + 1].reshape(batch, seq, LANES) for g in range(N_PAT)]
    return o, lse


def _attn_out_kernel(o0_ref, o1_ref, o2_ref, l0_ref, l1_ref, l2_ref, x_ref, mod_ref, w_out_ref,
                     ln_g_ref, ln_b_ref, out_ref):
    lses = [l0_ref[...], l1_ref[...], l2_ref[...]]
    mx = jnp.maximum(jnp.maximum(lses[0], lses[1]), lses[2])
    es = [jnp.exp(l - mx) for l in lses]
    den = es[0] + es[1] + es[2]
    head_of_lane = lax.broadcasted_iota(jnp.int32, (LANES, D_MODEL), 1) // B_HEAD_DIM
    expand = (head_of_lane == lax.broadcasted_iota(jnp.int32, (LANES, D_MODEL), 0)).astype(F32)
    o = None
    for e, o_ref in zip(es, (o0_ref, o1_ref, o2_ref)):
        w = jnp.dot(e / den, expand, preferred_element_type=F32,
                    precision=lax.Precision.HIGHEST)
        term = w * o_ref[...].astype(F32)
        o = term if o is None else o + term
    y = jnp.dot(o.astype(BF16), w_out_ref[...], preferred_element_type=F32)
    mod = mod_ref[...]
    out_ref[...] = _post_norm(x_ref[...], y, mod, ln_g_ref[...], ln_b_ref[...])


def _attn_out(o, lse, x, mod, w_out, ln_g, ln_b):
    batch, seq, _ = x.shape
    tm = 512
    row = lambda a: a.reshape(1, -1)
    tile = lambda w: pl.BlockSpec((None, tm, w), lambda b, i: (b, i, 0))
    return pl.pallas_call(
        _attn_out_kernel,
        grid=(batch, seq // tm),
        in_specs=[tile(D_MODEL)] * 3 + [tile(LANES)] * 3 + [
            tile(D_MODEL),
            pl.BlockSpec((None, 3, D_MODEL), lambda b, i: (b, 0, 0)),
            _resident((D_MODEL, D_MODEL)),
            _resident((1, D_MODEL)),
            _resident((1, D_MODEL)),
        ],
        out_specs=tile(D_MODEL),
        out_shape=jax.ShapeDtypeStruct(x.shape, F32),
        compiler_params=pltpu.CompilerParams(
            dimension_semantics=("arbitrary", "arbitrary"),
            vmem_limit_bytes=VMEM_LIMIT_BYTES),
        name="attn_out",
    )(*o, *lse, x, mod, w_out.astype(BF16), row(ln_g), row(ln_b))


def kernel(x, c, ada_w, ada_b, ln_g, ln_b, a_w_in, a_b_in, a_vn_g, a_vn_b, a_w_s, a_b_s, a_w_out,
           b_w_qkv, b_w_out, mlp_w_up, mlp_w_down):
    mods = _ada_mod(c, ada_w, ada_b)
    for i in range(DEPTH):
        j = i // 2
        if i % 2 == 0:
            x = _mixer_a(x, mods[2 * i], a_w_in[j], a_b_in[j], a_vn_g[j], a_vn_b[j], a_w_s[j],
                         a_b_s[j], a_w_out[j], ln_g[i, 0], ln_b[i, 0])
        else:
            qkv = _qkv(x, mods[2 * i], b_w_qkv[j])
            o, lse = _attention(qkv)
            x = _attn_out(o, lse, x, mods[2 * i], b_w_out[j], ln_g[i, 0], ln_b[i, 0])
        x = _mlp(x, mods[2 * i + 1], mlp_w_up[i], mlp_w_down[i], ln_g[i, 1], ln_b[i, 1])
    return x
```

```python
import functools

import jax
import jax.numpy as jnp
from jax import lax
from jax.experimental import pallas as pl
from jax.experimental.pallas import tpu as pltpu

D_MODEL = 1024
DEPTH = 2
CHUNK = 128
A_GROUPS = 16
A_GROUP_DIM = D_MODEL // A_GROUPS
B_HEADS = 16
B_HEAD_DIM = D_MODEL // B_HEADS
B_PATTERNS = ((128, 1), (512, 4), (2048, 16))
N_PAT = len(B_PATTERNS)
SPAN = 128
D_FF = 4 * D_MODEL
ALPHA = (2 * DEPTH) ** 0.25
LN_EPS = 1e-5
NEG = -1e30

LANES = 128
SUBLANES = 8
VMEM_LIMIT_BYTES = 56 * 1024 * 1024

RES = 16
MID = 4
SUB = RES // MID

F32 = jnp.float32
BF16 = jnp.bfloat16


def _layer_norm(x, g, b):
    mu = jnp.mean(x, axis=-1, keepdims=True)
    xc = x - mu
    var = jnp.mean(xc * xc, axis=-1, keepdims=True)
    return xc * lax.rsqrt(var + LN_EPS) * g + b


def _resident(shape):
    zeros = (0,) * len(shape)
    return pl.BlockSpec(shape, lambda *_: zeros, pipeline_mode=pl.Buffered(1))


def _ada_kernel(c_ref, w_ref, b_ref, o_ref):
    s = jax.nn.silu(c_ref[...])
    o_ref[0] = jnp.dot(s, w_ref[0], preferred_element_type=F32,
                       precision=lax.Precision.HIGHEST) + b_ref[0]


def _ada_mod(c, ada_w, ada_b):
    batch = c.shape[0]
    rows = SUBLANES
    c_pad = jnp.zeros((rows, D_MODEL), F32).at[:batch].set(c)
    n_sub = DEPTH * 2
    w = ada_w.reshape(n_sub, D_MODEL, 3 * D_MODEL)
    b = ada_b.reshape(n_sub, 1, 3 * D_MODEL)
    tn = 1024
    out = pl.pallas_call(
        _ada_kernel,
        grid=(n_sub, 3 * D_MODEL // tn),
        in_specs=[
            pl.BlockSpec((rows, D_MODEL), lambda i, j: (0, 0)),
            pl.BlockSpec((1, D_MODEL, tn), lambda i, j: (i, 0, j)),
            pl.BlockSpec((1, 1, tn), lambda i, j: (i, 0, j)),
        ],
        out_specs=pl.BlockSpec((1, rows, tn), lambda i, j: (i, 0, j)),
        out_shape=jax.ShapeDtypeStruct((n_sub, rows, 3 * D_MODEL), F32),
        name="ada_mod",
    )(c_pad, w, b)
    return out[:, :batch].reshape(n_sub, batch, 3, D_MODEL)


def _modulate(x, mod):
    shift = mod[0:1]
    scale = mod[1:2]
    return (x * (1.0 + scale) + shift).astype(BF16)


def _post_norm(x, y, mod, g, b):
    gate = 1.0 + mod[2:3]
    return _layer_norm(ALPHA * x + gate * y, g, b)


def _mixer_a_kernel(x_ref, mod_ref, w_in_ref, b_in_ref, vn_g_ref, vn_b_ref, ws_ref, bs_ref,
                    w_out_ref, ln_g_ref, ln_b_ref, o_ref, ws_scr, uv_scr, v_scr, g_scr):
    tm = x_ref.shape[0]

    @pl.when(jnp.logical_and(pl.program_id(0) == 0, pl.program_id(1) == 0))
    def _():
        t = lax.broadcasted_iota(jnp.int32, (CHUNK, CHUNK), 0)
        s = lax.broadcasted_iota(jnp.int32, (CHUNK, CHUNK), 1)
        causal = s <= t
        for g in range(A_GROUPS):
            ws_scr[g] = jnp.where(causal, ws_ref[g], 0.0).astype(BF16)

    x = x_ref[...]
    mod = mod_ref[...]
    h = _modulate(x, mod)
    uv = jnp.dot(h, w_in_ref[...], preferred_element_type=F32) + b_in_ref[...]
    uv_scr[...] = jax.nn.gelu(uv)
    v = _layer_norm(uv_scr[:, D_MODEL:], vn_g_ref[...], vn_b_ref[...])
    v_scr[...] = v.astype(BF16)

    low_half = lax.broadcasted_iota(jnp.int32, (CHUNK, LANES), 1) < A_GROUP_DIM

    def chunk_body(c, carry):
        r0 = pl.multiple_of(c * CHUNK, CHUNK)
        for gp in range(A_GROUPS // 2):
            cols = slice(gp * LANES, (gp + 1) * LANES)
            vb = v_scr[pl.ds(r0, CHUNK), cols]
            za = jnp.dot(ws_scr[2 * gp], vb, preferred_element_type=F32)
            zb = jnp.dot(ws_scr[2 * gp + 1], vb, preferred_element_type=F32)
            z = jnp.where(low_half, za, zb) + bs_ref[:, cols]
            u = uv_scr[pl.ds(r0, CHUNK), cols]
            g_scr[pl.ds(r0, CHUNK), cols] = (u * z).astype(BF16)
        return carry

    lax.fori_loop(0, tm // CHUNK, chunk_body, 0)

    y = jnp.dot(g_scr[...], w_out_ref[...], preferred_element_type=F32)
    o_ref[...] = _post_norm(x, y, mod, ln_g_ref[...], ln_b_ref[...])


def _mixer_a(x, mod, w_in, b_in, vn_g, vn_b, w_s, b_s, w_out, ln_g, ln_b):
    batch, seq, _ = x.shape
    tm = 512
    row = lambda a: a.reshape(1, -1)
    bs_plane = jnp.repeat(b_s.T, A_GROUP_DIM, axis=1)
    return pl.pallas_call(
        _mixer_a_kernel,
        grid=(batch, seq // tm),
        in_specs=[
            pl.BlockSpec((None, tm, D_MODEL), lambda b, i: (b, i, 0)),
            pl.BlockSpec((None, 3, D_MODEL), lambda b, i: (b, 0, 0)),
            _resident((D_MODEL, 2 * D_MODEL)),
            _resident((1, 2 * D_MODEL)),
            _resident((1, D_MODEL)),
            _resident((1, D_MODEL)),
            _resident((A_GROUPS, CHUNK, CHUNK)),
            _resident((CHUNK, D_MODEL)),
            _resident((D_MODEL, D_MODEL)),
            _resident((1, D_MODEL)),
            _resident((1, D_MODEL)),
        ],
        out_specs=pl.BlockSpec((None, tm, D_MODEL), lambda b, i: (b, i, 0)),
        out_shape=jax.ShapeDtypeStruct(x.shape, F32),
        scratch_shapes=[
            pltpu.VMEM((A_GROUPS, CHUNK, CHUNK), BF16),
            pltpu.VMEM((tm, 2 * D_MODEL), F32),
            pltpu.VMEM((tm, D_MODEL), BF16),
            pltpu.VMEM((tm, D_MODEL), BF16),
        ],
        compiler_params=pltpu.CompilerParams(
            dimension_semantics=("arbitrary", "arbitrary"),
            vmem_limit_bytes=VMEM_LIMIT_BYTES),
        name="mixer_a",
    )(x, mod, w_in.astype(BF16), row(b_in), row(vn_g), row(vn_b), w_s, bs_plane,
      w_out.astype(BF16), row(ln_g), row(ln_b))


def _mlp_kernel(x_ref, mod_ref, w_up_ref, w_down_ref, ln_g_ref, ln_b_ref, o_ref):
    x = x_ref[...]
    mod = mod_ref[...]
    h = _modulate(x, mod)
    a = jnp.dot(h, w_up_ref[...], preferred_element_type=F32)
    r = jnp.square(jnp.maximum(a, 0.0)).astype(BF16)
    y = jnp.dot(r, w_down_ref[...], preferred_element_type=F32)
    o_ref[...] = _post_norm(x, y, mod, ln_g_ref[...], ln_b_ref[...])


def _mlp(x, mod, w_up, w_down, ln_g, ln_b):
    batch, seq, _ = x.shape
    tm = 512
    row = lambda a: a.reshape(1, -1)
    return pl.pallas_call(
        _mlp_kernel,
        grid=(batch, seq // tm),
        in_specs=[
            pl.BlockSpec((None, tm, D_MODEL), lambda b, i: (b, i, 0)),
            pl.BlockSpec((None, 3, D_MODEL), lambda b, i: (b, 0, 0)),
            _resident((D_MODEL, D_FF)),
            _resident((D_FF, D_MODEL)),
            _resident((1, D_MODEL)),
            _resident((1, D_MODEL)),
        ],
        out_specs=pl.BlockSpec((None, tm, D_MODEL), lambda b, i: (b, i, 0)),
        out_shape=jax.ShapeDtypeStruct(x.shape, F32),
        compiler_params=pltpu.CompilerParams(
            dimension_semantics=("arbitrary", "arbitrary"),
            vmem_limit_bytes=VMEM_LIMIT_BYTES),
        name="mlp",
    )(x, mod, w_up.astype(BF16), w_down.astype(BF16), row(ln_g), row(ln_b))


def _permute_kernel(to_residue, x_hbm, o_hbm, sem):
    batch = x_hbm.shape[0]
    copies = []
    for b in range(batch):
        for r in range(RES):
            if to_residue:
                src, dst = x_hbm.at[b, :, r, :], o_hbm.at[b, r]
            else:
                src, dst = x_hbm.at[b, r], o_hbm.at[b, :, r, :]
            copies.append(pltpu.make_async_copy(src, dst, sem.at[b, r]))
    for cp in copies:
        cp.start()
    for cp in copies:
        cp.wait()


def _permute(x, to_residue):
    batch, seq, dim = x.shape
    nat = (batch, seq // RES, RES, dim)
    res = (batch, RES, seq // RES, dim)
    src, dst = (nat, res) if to_residue else (res, nat)
    out = pl.pallas_call(
        functools.partial(_permute_kernel, to_residue),
        in_specs=[pl.BlockSpec(memory_space=pl.ANY)],
        out_specs=pl.BlockSpec(memory_space=pl.ANY),
        out_shape=jax.ShapeDtypeStruct(dst, x.dtype),
        scratch_shapes=[pltpu.SemaphoreType.DMA((batch, RES))],
        name="to_residue_order" if to_residue else "to_natural_order",
    )(x.reshape(src))
    return out.reshape(batch, seq, dim)


def _qkv_kernel(x_nat_ref, x_res_ref, mod_ref, w_ref, o_ref, h_scr):
    j = pl.program_id(2)

    @pl.when(j == 0)
    def _():
        h_scr[...] = _modulate(x_nat_ref[...], mod_ref[...])

    @pl.when(j == 2)
    def _():
        h_scr[...] = _modulate(x_res_ref[...], mod_ref[...])

    o_ref[...] = jnp.dot(h_scr[...], w_ref[...], preferred_element_type=F32).astype(BF16)


def _qkv_col_block(j):
    return jnp.where(j < 2, j + 1, jnp.where(j == 2, 0, j))


def _qkv(x_nat, x_res, mod, w_qkv):
    batch, seq, _ = x_nat.shape
    n_out = w_qkv.shape[1]
    tm, tn = 1024, B_HEADS * B_HEAD_DIM
    return pl.pallas_call(
        _qkv_kernel,
        grid=(batch, seq // tm, n_out // tn),
        in_specs=[
            pl.BlockSpec((None, tm, D_MODEL), lambda b, i, j: (b, i, 0)),
            pl.BlockSpec((None, tm, D_MODEL), lambda b, i, j: (b, i, 0)),
            pl.BlockSpec((None, 3, D_MODEL), lambda b, i, j: (b, 0, 0)),
            pl.BlockSpec((D_MODEL, tn), lambda b, i, j: (0, _qkv_col_block(j))),
        ],
        out_specs=pl.BlockSpec((None, tm, tn), lambda b, i, j: (b, i, _qkv_col_block(j))),
        out_shape=jax.ShapeDtypeStruct((batch, seq, n_out), BF16),
        scratch_shapes=[pltpu.VMEM((tm, D_MODEL), BF16)],
        compiler_params=pltpu.CompilerParams(
            dimension_semantics=("arbitrary", "arbitrary", "arbitrary"),
            vmem_limit_bytes=VMEM_LIMIT_BYTES),
        name="qkv_proj",
    )(x_nat, x_res, mod, w_qkv.astype(BF16))


HALF = SPAN // RES
QUARTER = SPAN // SUB


def _build_bias_tables(bias_scr):
    row = lax.broadcasted_iota(jnp.int32, (SPAN, 2 * SPAN), 0)
    col = lax.broadcasted_iota(jnp.int32, (SPAN, 2 * SPAN), 1)
    diff_local = SPAN + RES * (row % HALF) + row // HALF - col
    diff_mid = (SUB * (row % QUARTER) + row // QUARTER
                - SUB * (col % (2 * QUARTER) - QUARTER) - col // (2 * QUARTER))
    diff_wide = SPAN + row - col
    for g, diff in enumerate((diff_local, diff_mid, diff_wide)):
        dilation = B_PATTERNS[g][1]
        valid = jnp.logical_and(diff >= 0, diff <= SPAN)
        dist = (dilation * diff).astype(F32)
        for hd in range(B_HEADS):
            slope = 2.0 ** (-8.0 * (hd + 1) / B_HEADS)
            bias_scr[g, hd] = jnp.where(valid, -slope * dist, NEG)


def _attn_kernel(q1_ref, k1p_ref, k1c_ref, v1p_ref, v1c_ref,
                 q2_ref, k2p_ref, k2c_ref, v2p_ref, v2c_ref,
                 q3_ref, k3_ref, v3_ref,
                 o1_ref, l1_ref, o2_ref, l2_ref, o3_ref, l3_ref, bias_scr):
    step = pl.program_id(1)

    @pl.when(jnp.logical_and(pl.program_id(0) == 0, step == 0))
    def _():
        _build_bias_tables(bias_scr)

    lane = lax.broadcasted_iota(jnp.int32, (SPAN, LANES), 1)
    low_half = lane < B_HEAD_DIM
    scale = B_HEAD_DIM ** -0.5
    q_mask = (jnp.where(low_half, scale, 0.0).astype(BF16),
              jnp.where(low_half, 0.0, scale).astype(BF16))

    key_lane = lax.broadcasted_iota(jnp.int32, (1, 2 * SPAN), 1)
    pen_prev_half = jnp.where(key_lane < SPAN, NEG, 0.0).astype(F32)
    pen_prev_pieces = jnp.where(key_lane % (2 * QUARTER) < QUARTER, NEG, 0.0).astype(F32)
    first_local = (step == 0).astype(F32)
    mid_steps = pl.num_programs(1) // MID
    first_mid = (step % mid_steps == 0).astype(F32)

    def unit(q2, k2, v2, g, hp, pen):
        outs, lses = [], []
        for j in range(2):
            hd = 2 * hp + j
            s = lax.dot_general(q2 * q_mask[j], k2, (((1,), (1,)), ((), ())),
                                preferred_element_type=F32)
            s = s + bias_scr[g, hd]
            if pen is not None:
                s = s + pen
            m = jnp.max(s, axis=-1, keepdims=True)
            p = jnp.exp(s - m)
            l = jnp.sum(p, axis=-1, keepdims=True)
            pv = jnp.dot(p.astype(BF16), v2, preferred_element_type=F32)
            outs.append(pv / l)
            lses.append(m + jnp.log(l))
        o = jnp.where(low_half, outs[0], outs[1])
        return o, lses

    def pair_body(hp, carry):
        l1a, l1b, l2a, l2b, l3a, l3b = carry
        cols = pl.ds(pl.multiple_of(hp * LANES, LANES), LANES)

        def put(tile, lses):
            for j in range(2):
                tile = jnp.where(lane == 2 * hp + j, lses[j], tile)
            return tile

        q1 = q1_ref[:, :, cols].astype(F32).reshape(RES, 2, HALF, LANES)
        q1a = q1[:, 0].reshape(SPAN, LANES).astype(BF16)
        q1b = q1[:, 1].reshape(SPAN, LANES).astype(BF16)
        k1c = k1c_ref[:, cols]
        v1c = v1c_ref[:, cols]
        oa, la = unit(q1a, jnp.concatenate([k1p_ref[:, cols], k1c[:SPAN]], axis=0),
                      jnp.concatenate([v1p_ref[:, cols], v1c[:SPAN]], axis=0),
                      0, hp, first_local * pen_prev_half)
        ob, lb = unit(q1b, k1c, v1c, 0, hp, None)
        o1 = jnp.concatenate([oa.reshape(RES, HALF, LANES), ob.reshape(RES, HALF, LANES)], axis=1)
        o1_ref[:, :, cols] = o1.astype(BF16)
        l1a, l1b = put(l1a, la), put(l1b, lb)

        q2 = q2_ref[:, :, cols]
        k2p, k2c = k2p_ref[:, :, cols], k2c_ref[:, :, cols]
        v2p, v2c = v2p_ref[:, :, cols], v2c_ref[:, :, cols]
        q2a = q2[:, :QUARTER].reshape(SPAN, LANES)
        q2b = q2[:, QUARTER:].reshape(SPAN, LANES)
        k2a = jnp.concatenate([k2p, k2c[:, :QUARTER]], axis=1).reshape(2 * SPAN, LANES)
        v2a = jnp.concatenate([v2p, v2c[:, :QUARTER]], axis=1).reshape(2 * SPAN, LANES)
        oa, la = unit(q2a, k2a, v2a, 1, hp, first_mid * pen_prev_pieces)
        ob, lb = unit(q2b, k2c.reshape(2 * SPAN, LANES), v2c.reshape(2 * SPAN, LANES), 1, hp, None)
        o2_ref[:, :QUARTER, cols] = oa.reshape(SUB, QUARTER, LANES).astype(BF16)
        o2_ref[:, QUARTER:, cols] = ob.reshape(SUB, QUARTER, LANES).astype(BF16)
        l2a, l2b = put(l2a, la), put(l2b, lb)

        q3, k3, v3 = q3_ref[:, cols], k3_ref[:, cols], v3_ref[:, cols]
        oa, la = unit(q3[:SPAN], jnp.concatenate([k3[:SPAN], k3[:SPAN]], axis=0),
                      jnp.concatenate([v3[:SPAN], v3[:SPAN]], axis=0), 2, hp, pen_prev_half)
        ob, lb = unit(q3[SPAN:], k3, v3, 2, hp, None)
        o3_ref[:SPAN, cols] = oa.astype(BF16)
        o3_ref[SPAN:, cols] = ob.astype(BF16)
        l3a, l3b = put(l3a, la), put(l3b, lb)
        return l1a, l1b, l2a, l2b, l3a, l3b

    zero = jnp.zeros((SPAN, LANES), F32)
    l1a, l1b, l2a, l2b, l3a, l3b = lax.fori_loop(0, B_HEADS // 2, pair_body, (zero,) * 6)
    l1_ref[...] = jnp.concatenate(
        [l1a.reshape(RES, HALF, LANES), l1b.reshape(RES, HALF, LANES)], axis=1)
    l2_ref[:, :QUARTER] = l2a.reshape(SUB, QUARTER, LANES)
    l2_ref[:, QUARTER:] = l2b.reshape(SUB, QUARTER, LANES)
    l3_ref[:SPAN] = l3a
    l3_ref[SPAN:] = l3b


def _attention(qkv):
    batch, seq, n_qkv = qkv.shape
    width = B_HEADS * B_HEAD_DIM
    per = seq // RES
    n_steps = seq // (2 * SPAN)
    by_res = qkv.reshape(batch, RES, per, n_qkv)
    by_sub = qkv.reshape(batch, SUB, MID, per, n_qkv)

    def col(g, part):
        return g * 3 + part

    mid_steps = n_steps // MID

    def mid_spec(rows, part, prev):
        def index_map(b, s):
            blk = s % mid_steps
            if prev:
                blk = jnp.maximum(2 * blk - 1, 0)
            return (b, 0, s // mid_steps, blk, col(1, part))
        return pl.BlockSpec((None, SUB, None, rows, width), index_map)

    in_specs = [
        pl.BlockSpec((None, RES, 2 * HALF, width), lambda b, s: (b, 0, s, col(0, 0))),
        pl.BlockSpec((None, SPAN, width), lambda b, s: (b, jnp.maximum(2 * s - 1, 0), col(0, 1))),
        pl.BlockSpec((None, 2 * SPAN, width), lambda b, s: (b, s, col(0, 1))),
        pl.BlockSpec((None, SPAN, width), lambda b, s: (b, jnp.maximum(2 * s - 1, 0), col(0, 2))),
        pl.BlockSpec((None, 2 * SPAN, width), lambda b, s: (b, s, col(0, 2))),
        mid_spec(2 * QUARTER, 0, False),
        mid_spec(QUARTER, 1, True), mid_spec(2 * QUARTER, 1, False),
        mid_spec(QUARTER, 2, True), mid_spec(2 * QUARTER, 2, False),
        pl.BlockSpec((None, None, per, width), lambda b, s: (b, s, 0, col(2, 0))),
        pl.BlockSpec((None, None, per, width), lambda b, s: (b, s, 0, col(2, 1))),
        pl.BlockSpec((None, None, per, width), lambda b, s: (b, s, 0, col(2, 2))),
    ]
    operands = [by_res, qkv, qkv, qkv, qkv] + [by_sub] * 5 + [by_res] * 3

    out_specs, out_shapes = [], []

    def outs_for(w, dtype):
        return (
            [pl.BlockSpec((None, RES, 2 * HALF, w), lambda b, s: (b, 0, s, 0)),
             pl.BlockSpec((None, SUB, None, 2 * QUARTER, w),
                          lambda b, s: (b, 0, s // mid_steps, s % mid_steps, 0)),
             pl.BlockSpec((None, None, per, w), lambda b, s: (b, s, 0, 0))],
            [jax.ShapeDtypeStruct((batch, RES, per, w), dtype),
             jax.ShapeDtypeStruct((batch, SUB, MID, per, w), dtype),
             jax.ShapeDtypeStruct((batch, RES, per, w), dtype)])
    o_specs, o_shapes = outs_for(width, BF16)
    l_specs, l_shapes = outs_for(LANES, F32)
    for g in range(N_PAT):
        out_specs += [o_specs[g], l_specs[g]]
        out_shapes += [o_shapes[g], l_shapes[g]]

    outs = pl.pallas_call(
        _attn_kernel,
        grid=(batch, n_steps),
        in_specs=in_specs,
        out_specs=out_specs,
        out_shape=out_shapes,
        scratch_shapes=[pltpu.VMEM((N_PAT, B_HEADS, SPAN, 2 * SPAN), F32)],
        compiler_params=pltpu.CompilerParams(
            dimension_semantics=("arbitrary", "arbitrary"),
            vmem_limit_bytes=VMEM_LIMIT_BYTES),
        name="dilated_attn",
    )(*operands)
    o = [outs[2 * g].reshape(batch, seq, width) for g in range(N_PAT)]
    lse = [outs[2 * g + 1].reshape(batch, seq, LANES) for g in range(N_PAT)]
    return o, lse


def _attn_out_kernel(o0_ref, o1_ref, o2_ref, l0_ref, l1_ref, l2_ref, x_ref, mod_ref, w_out_ref,
                     ln_g_ref, ln_b_ref, out_ref):
    lses = [l0_ref[...], l1_ref[...], l2_ref[...]]
    mx = jnp.maximum(jnp.maximum(lses[0], lses[1]), lses[2])
    es = [jnp.exp(l - mx) for l in lses]
    den = es[0] + es[1] + es[2]
    head_of_lane = lax.broadcasted_iota(jnp.int32, (LANES, D_MODEL), 1) // B_HEAD_DIM
    expand = (head_of_lane == lax.broadcasted_iota(jnp.int32, (LANES, D_MODEL), 0)).astype(F32)
    o = None
    for e, o_ref in zip(es, (o0_ref, o1_ref, o2_ref)):
        w = jnp.dot(e / den, expand, preferred_element_type=F32,
                    precision=lax.Precision.HIGHEST)
        term = w * o_ref[...].astype(F32)
        o = term if o is None else o + term
    y = jnp.dot(o.astype(BF16), w_out_ref[...], preferred_element_type=F32)
    mod = mod_ref[...]
    out_ref[...] = _post_norm(x_ref[...], y, mod, ln_g_ref[...], ln_b_ref[...])


def _attn_out(o, lse, x, mod, w_out, ln_g, ln_b):
    batch, seq, _ = x.shape
    tm = 512
    row = lambda a: a.reshape(1, -1)
    tile = lambda w: pl.BlockSpec((None, tm, w), lambda b, i: (b, i, 0))
    return pl.pallas_call(
        _attn_out_kernel,
        grid=(batch, seq // tm),
        in_specs=[tile(D_MODEL)] * 3 + [tile(LANES)] * 3 + [
            tile(D_MODEL),
            pl.BlockSpec((None, 3, D_MODEL), lambda b, i: (b, 0, 0)),
            _resident((D_MODEL, D_MODEL)),
            _resident((1, D_MODEL)),
            _resident((1, D_MODEL)),
        ],
        out_specs=tile(D_MODEL),
        out_shape=jax.ShapeDtypeStruct(x.shape, F32),
        compiler_params=pltpu.CompilerParams(
            dimension_semantics=("arbitrary", "arbitrary"),
            vmem_limit_bytes=VMEM_LIMIT_BYTES),
        name="attn_out",
    )(*o, *lse, x, mod, w_out.astype(BF16), row(ln_g), row(ln_b))


def kernel(x, c, ada_w, ada_b, ln_g, ln_b, a_w_in, a_b_in, a_vn_g, a_vn_b, a_w_s, a_b_s, a_w_out,
           b_w_qkv, b_w_out, mlp_w_up, mlp_w_down):
    mods = _ada_mod(c, ada_w, ada_b)
    x = _mixer_a(x, mods[0], a_w_in[0], a_b_in[0], a_vn_g[0], a_vn_b[0], a_w_s[0], a_b_s[0],
                 a_w_out[0], ln_g[0, 0], ln_b[0, 0])
    x = _mlp(x, mods[1], mlp_w_up[0], mlp_w_down[0], ln_g[0, 1], ln_b[0, 1])
    x_res = _permute(x, to_residue=True)
    qkv = _qkv(x, x_res, mods[2], b_w_qkv[0])
    o, lse = _attention(qkv)
    x_res = _attn_out(o, lse, x_res, mods[2], b_w_out[0], ln_g[1, 0], ln_b[1, 0])
    x_res = _mlp(x_res, mods[3], mlp_w_up[1], mlp_w_down[1], ln_g[1, 1], ln_b[1, 1])
    return _permute(x_res, to_residue=False)
```

```python
import functools

import jax
import jax.numpy as jnp
from jax import lax
from jax.experimental import pallas as pl
from jax.experimental.pallas import tpu as pltpu

D_MODEL = 1024
DEPTH = 2
CHUNK = 128
A_GROUPS = 16
A_GROUP_DIM = D_MODEL // A_GROUPS
B_HEADS = 16
B_HEAD_DIM = D_MODEL // B_HEADS
B_PATTERNS = ((128, 1), (512, 4), (2048, 16))
N_PAT = len(B_PATTERNS)
SPAN = 128
D_FF = 4 * D_MODEL
ALPHA = (2 * DEPTH) ** 0.25
LN_EPS = 1e-5
NEG = -1e30

LANES = 128
SUBLANES = 8
VMEM_LIMIT_BYTES = 56 * 1024 * 1024

RES = 16
MID = 4
SUB = RES // MID

F32 = jnp.float32
BF16 = jnp.bfloat16


def _layer_norm(x, g, b):
    mu = jnp.mean(x, axis=-1, keepdims=True)
    xc = x - mu
    var = jnp.mean(xc * xc, axis=-1, keepdims=True)
    return xc * lax.rsqrt(var + LN_EPS) * g + b


def _resident(shape):
    zeros = (0,) * len(shape)
    return pl.BlockSpec(shape, lambda *_: zeros, pipeline_mode=pl.Buffered(1))


def _ada_kernel(c_ref, w_ref, b_ref, o_ref):
    s = jax.nn.silu(c_ref[...])
    o_ref[0] = jnp.dot(s, w_ref[0], preferred_element_type=F32,
                       precision=lax.Precision.HIGHEST) + b_ref[0]


def _ada_mod(c, ada_w, ada_b):
    batch = c.shape[0]
    rows = SUBLANES
    c_pad = jnp.zeros((rows, D_MODEL), F32).at[:batch].set(c)
    n_sub = DEPTH * 2
    w = ada_w.reshape(n_sub, D_MODEL, 3 * D_MODEL)
    b = ada_b.reshape(n_sub, 1, 3 * D_MODEL)
    tn = 1024
    out = pl.pallas_call(
        _ada_kernel,
        grid=(n_sub, 3 * D_MODEL // tn),
        in_specs=[
            pl.BlockSpec((rows, D_MODEL), lambda i, j: (0, 0)),
            pl.BlockSpec((1, D_MODEL, tn), lambda i, j: (i, 0, j)),
            pl.BlockSpec((1, 1, tn), lambda i, j: (i, 0, j)),
        ],
        out_specs=pl.BlockSpec((1, rows, tn), lambda i, j: (i, 0, j)),
        out_shape=jax.ShapeDtypeStruct((n_sub, rows, 3 * D_MODEL), F32),
        name="ada_mod",
    )(c_pad, w, b)
    return out[:, :batch].reshape(n_sub, batch, 3, D_MODEL)


def _modulate(x, mod):
    shift = mod[0:1]
    scale = mod[1:2]
    return (x * (1.0 + scale) + shift).astype(BF16)


def _post_norm(x, y, mod, g, b):
    gate = 1.0 + mod[2:3]
    return _layer_norm(ALPHA * x + gate * y, g, b)


def _mixer_a_kernel(x_ref, mod_ref, w_in_ref, b_in_ref, vn_g_ref, vn_b_ref, ws_ref, bs_ref,
                    w_out_ref, ln_g_ref, ln_b_ref, o_ref, ws_scr, uv_scr, v_scr, g_scr):
    tm = x_ref.shape[0]

    @pl.when(jnp.logical_and(pl.program_id(0) == 0, pl.program_id(1) == 0))
    def _():
        t = lax.broadcasted_iota(jnp.int32, (CHUNK, CHUNK), 0)
        s = lax.broadcasted_iota(jnp.int32, (CHUNK, CHUNK), 1)
        causal = s <= t
        for g in range(A_GROUPS):
            ws_scr[g] = jnp.where(causal, ws_ref[g], 0.0).astype(BF16)

    x = x_ref[...]
    mod = mod_ref[...]
    h = _modulate(x, mod)
    uv = jnp.dot(h, w_in_ref[...], preferred_element_type=F32) + b_in_ref[...]
    uv_scr[...] = jax.nn.gelu(uv)
    v = _layer_norm(uv_scr[:, D_MODEL:], vn_g_ref[...], vn_b_ref[...])
    v_scr[...] = v.astype(BF16)

    low_half = lax.broadcasted_iota(jnp.int32, (CHUNK, LANES), 1) < A_GROUP_DIM

    def chunk_body(c, carry):
        r0 = pl.multiple_of(c * CHUNK, CHUNK)
        for gp in range(A_GROUPS // 2):
            cols = slice(gp * LANES, (gp + 1) * LANES)
            vb = v_scr[pl.ds(r0, CHUNK), cols]
            za = jnp.dot(ws_scr[2 * gp], vb, preferred_element_type=F32)
            zb = jnp.dot(ws_scr[2 * gp + 1], vb, preferred_element_type=F32)
            z = jnp.where(low_half, za, zb) + bs_ref[:, cols]
            u = uv_scr[pl.ds(r0, CHUNK), cols]
            g_scr[pl.ds(r0, CHUNK), cols] = (u * z).astype(BF16)
        return carry

    lax.fori_loop(0, tm // CHUNK, chunk_body, 0)

    y = jnp.dot(g_scr[...], w_out_ref[...], preferred_element_type=F32)
    o_ref[...] = _post_norm(x, y, mod, ln_g_ref[...], ln_b_ref[...])


def _mixer_a(x, mod, w_in, b_in, vn_g, vn_b, w_s, b_s, w_out, ln_g, ln_b):
    batch, seq, _ = x.shape
    tm = 512
    row = lambda a: a.reshape(1, -1)
    bs_plane = jnp.repeat(b_s.T, A_GROUP_DIM, axis=1)
    return pl.pallas_call(
        _mixer_a_kernel,
        grid=(batch, seq // tm),
        in_specs=[
            pl.BlockSpec((None, tm, D_MODEL), lambda b, i: (b, i, 0)),
            pl.BlockSpec((None, 3, D_MODEL), lambda b, i: (b, 0, 0)),
            _resident((D_MODEL, 2 * D_MODEL)),
            _resident((1, 2 * D_MODEL)),
            _resident((1, D_MODEL)),
            _resident((1, D_MODEL)),
            _resident((A_GROUPS, CHUNK, CHUNK)),
            _resident((CHUNK, D_MODEL)),
            _resident((D_MODEL, D_MODEL)),
            _resident((1, D_MODEL)),
            _resident((1, D_MODEL)),
        ],
        out_specs=pl.BlockSpec((None, tm, D_MODEL), lambda b, i: (b, i, 0)),
        out_shape=jax.ShapeDtypeStruct(x.shape, F32),
        scratch_shapes=[
            pltpu.VMEM((A_GROUPS, CHUNK, CHUNK), BF16),
            pltpu.VMEM((tm, 2 * D_MODEL), F32),
            pltpu.VMEM((tm, D_MODEL), BF16),
            pltpu.VMEM((tm, D_MODEL), BF16),
        ],
        compiler_params=pltpu.CompilerParams(
            dimension_semantics=("arbitrary", "arbitrary"),
            vmem_limit_bytes=VMEM_LIMIT_BYTES),
        name="mixer_a",
    )(x, mod, w_in.astype(BF16), row(b_in), row(vn_g), row(vn_b), w_s, bs_plane,
      w_out.astype(BF16), row(ln_g), row(ln_b))


def _mlp_kernel(residue_in, x_ref, mod_ref, w_up_ref, w_down_ref, ln_g_ref, ln_b_ref, *o_refs):
    tm = o_refs[0].shape[0]
    x = x_ref[...].reshape(tm, D_MODEL)
    mod = mod_ref[...]
    h = _modulate(x, mod)
    a = jnp.dot(h, w_up_ref[...], preferred_element_type=F32)
    r = jnp.square(jnp.maximum(a, 0.0)).astype(BF16)
    y = jnp.dot(r, w_down_ref[...], preferred_element_type=F32)
    out = _post_norm(x, y, mod, ln_g_ref[...], ln_b_ref[...])
    if residue_in:
        (nat_ref,) = o_refs
        nat_ref[...] = jnp.swapaxes(out.reshape(RES, tm // RES, D_MODEL), 0, 1).reshape(tm, D_MODEL)
    else:
        nat_ref, res_ref = o_refs
        nat_ref[...] = out
        res_ref[...] = jnp.swapaxes(out.reshape(tm // RES, RES, D_MODEL), 0, 1)


def _mlp(x, mod, w_up, w_down, ln_g, ln_b, residue_in):
    batch, seq, _ = x.shape
    tm = 512
    row = lambda a: a.reshape(1, -1)
    nat_spec = pl.BlockSpec((None, tm, D_MODEL), lambda b, i: (b, i, 0))
    res_spec = pl.BlockSpec((None, RES, tm // RES, D_MODEL), lambda b, i: (b, 0, i, 0))
    nat_shape = jax.ShapeDtypeStruct((batch, seq, D_MODEL), F32)
    res_shape = jax.ShapeDtypeStruct((batch, RES, seq // RES, D_MODEL), F32)
    if residue_in:
        x = x.reshape(res_shape.shape)
    outs = pl.pallas_call(
        functools.partial(_mlp_kernel, residue_in),
        grid=(batch, seq // tm),
        in_specs=[
            res_spec if residue_in else nat_spec,
            pl.BlockSpec((None, 3, D_MODEL), lambda b, i: (b, 0, 0)),
            _resident((D_MODEL, D_FF)),
            _resident((D_FF, D_MODEL)),
            _resident((1, D_MODEL)),
            _resident((1, D_MODEL)),
        ],
        out_specs=[nat_spec] if residue_in else [nat_spec, res_spec],
        out_shape=[nat_shape] if residue_in else [nat_shape, res_shape],
        compiler_params=pltpu.CompilerParams(
            dimension_semantics=("arbitrary", "arbitrary"),
            vmem_limit_bytes=VMEM_LIMIT_BYTES),
        name="mlp",
    )(x, mod, w_up.astype(BF16), w_down.astype(BF16), row(ln_g), row(ln_b))
    if residue_in:
        return outs[0]
    return outs[0], outs[1].reshape(batch, seq, D_MODEL)


def _qkv_kernel(x_nat_ref, x_res_ref, mod_ref, w_ref, o_ref, h_scr):
    j = pl.program_id(2)

    @pl.when(j == 0)
    def _():
        h_scr[...] = _modulate(x_nat_ref[...], mod_ref[...])

    @pl.when(j == 2)
    def _():
        h_scr[...] = _modulate(x_res_ref[...], mod_ref[...])

    o_ref[...] = jnp.dot(h_scr[...], w_ref[...], preferred_element_type=F32).astype(BF16)


def _qkv_col_block(j):
    return jnp.where(j < 2, j + 1, jnp.where(j == 2, 0, j))


def _qkv(x_nat, x_res, mod, w_qkv):
    batch, seq, _ = x_nat.shape
    n_out = w_qkv.shape[1]
    tm, tn = 1024, B_HEADS * B_HEAD_DIM
    return pl.pallas_call(
        _qkv_kernel,
        grid=(batch, seq // tm, n_out // tn),
        in_specs=[
            pl.BlockSpec((None, tm, D_MODEL), lambda b, i, j: (b, i, 0)),
            pl.BlockSpec((None, tm, D_MODEL), lambda b, i, j: (b, i, 0)),
            pl.BlockSpec((None, 3, D_MODEL), lambda b, i, j: (b, 0, 0)),
            pl.BlockSpec((D_MODEL, tn), lambda b, i, j: (0, _qkv_col_block(j))),
        ],
        out_specs=pl.BlockSpec((None, tm, tn), lambda b, i, j: (b, i, _qkv_col_block(j))),
        out_shape=jax.ShapeDtypeStruct((batch, seq, n_out), BF16),
        scratch_shapes=[pltpu.VMEM((tm, D_MODEL), BF16)],
        compiler_params=pltpu.CompilerParams(
            dimension_semantics=("arbitrary", "arbitrary", "arbitrary"),
            vmem_limit_bytes=VMEM_LIMIT_BYTES),
        name="qkv_proj",
    )(x_nat, x_res, mod, w_qkv.astype(BF16))


HALF = SPAN // RES
QUARTER = SPAN // SUB


def _build_bias_tables(bias_scr):
    row = lax.broadcasted_iota(jnp.int32, (SPAN, 2 * SPAN), 0)
    col = lax.broadcasted_iota(jnp.int32, (SPAN, 2 * SPAN), 1)
    diff_local = SPAN + RES * (row % HALF) + row // HALF - col
    diff_mid = (SUB * (row % QUARTER) + row // QUARTER
                - SUB * (col % (2 * QUARTER) - QUARTER) - col // (2 * QUARTER))
    diff_wide = SPAN + row - col
    for g, diff in enumerate((diff_local, diff_mid, diff_wide)):
        dilation = B_PATTERNS[g][1]
        valid = jnp.logical_and(diff >= 0, diff <= SPAN)
        dist = (dilation * diff).astype(F32)
        for hd in range(B_HEADS):
            slope = 2.0 ** (-8.0 * (hd + 1) / B_HEADS)
            bias_scr[g, hd] = jnp.where(valid, -slope * dist, NEG)


def _attn_kernel(q1_ref, k1p_ref, k1c_ref, v1p_ref, v1c_ref,
                 q2_ref, k2p_ref, k2c_ref, v2p_ref, v2c_ref,
                 q3_ref, k3_ref, v3_ref,
                 o1_ref, l1_ref, o2_ref, l2_ref, o3_ref, l3_ref, bias_scr):
    step = pl.program_id(1)

    @pl.when(jnp.logical_and(pl.program_id(0) == 0, step == 0))
    def _():
        _build_bias_tables(bias_scr)

    lane = lax.broadcasted_iota(jnp.int32, (SPAN, LANES), 1)
    low_half = lane < B_HEAD_DIM
    lane_head = lane % B_HEADS
    scale = B_HEAD_DIM ** -0.5
    q_mask = (jnp.where(low_half, scale, 0.0).astype(BF16),
              jnp.where(low_half, 0.0, scale).astype(BF16))

    key_lane = lax.broadcasted_iota(jnp.int32, (1, 2 * SPAN), 1)
    pen_prev_half = jnp.where(key_lane < SPAN, NEG, 0.0).astype(F32)
    pen_prev_pieces = jnp.where(key_lane % (2 * QUARTER) < QUARTER, NEG, 0.0).astype(F32)
    first_local = (step == 0).astype(F32)
    mid_steps = pl.num_programs(1) // MID
    first_mid = (step % mid_steps == 0).astype(F32)

    def unit(q2, k2, v2, g, hp, pen):
        outs, lses = [], []
        for j in range(2):
            hd = 2 * hp + j
            s = lax.dot_general(q2 * q_mask[j], k2, (((1,), (1,)), ((), ())),
                                preferred_element_type=F32)
            s = s + bias_scr[g, hd]
            if pen is not None:
                s = s + pen
            m = jnp.max(s, axis=-1, keepdims=True)
            p = jnp.exp(s - m)
            l = jnp.sum(p, axis=-1, keepdims=True)
            pv = jnp.dot(p.astype(BF16), v2, preferred_element_type=F32)
            outs.append(pv / l)
            lses.append(m + jnp.log(l))
        o = jnp.where(low_half, outs[0], outs[1])
        return o, lses

    def pair_body(hp, carry):
        l1a, l1b, l2a, l2b, l3a, l3b = carry
        cols = pl.ds(pl.multiple_of(hp * LANES, LANES), LANES)

        def put(tile, lses):
            for j in range(2):
                tile = jnp.where(lane_head == 2 * hp + j, lses[j], tile)
            return tile

        q1 = q1_ref[:, :, cols].astype(F32).reshape(RES, 2, HALF, LANES)
        q1a = q1[:, 0].reshape(SPAN, LANES).astype(BF16)
        q1b = q1[:, 1].reshape(SPAN, LANES).astype(BF16)
        k1c = k1c_ref[:, cols]
        v1c = v1c_ref[:, cols]
        oa, la = unit(q1a, jnp.concatenate([k1p_ref[:, cols], k1c[:SPAN]], axis=0),
                      jnp.concatenate([v1p_ref[:, cols], v1c[:SPAN]], axis=0),
                      0, hp, first_local * pen_prev_half)
        ob, lb = unit(q1b, k1c, v1c, 0, hp, None)
        o1 = jnp.concatenate([oa.reshape(RES, HALF, LANES), ob.reshape(RES, HALF, LANES)], axis=1)
        o1_ref[:, :, cols] = o1.astype(BF16)
        l1a, l1b = put(l1a, la), put(l1b, lb)

        q2 = q2_ref[:, :, cols]
        k2p, k2c = k2p_ref[:, :, cols], k2c_ref[:, :, cols]
        v2p, v2c = v2p_ref[:, :, cols], v2c_ref[:, :, cols]
        q2a = q2[:, :QUARTER].reshape(SPAN, LANES)
        q2b = q2[:, QUARTER:].reshape(SPAN, LANES)
        k2a = jnp.concatenate([k2p, k2c[:, :QUARTER]], axis=1).reshape(2 * SPAN, LANES)
        v2a = jnp.concatenate([v2p, v2c[:, :QUARTER]], axis=1).reshape(2 * SPAN, LANES)
        oa, la = unit(q2a, k2a, v2a, 1, hp, first_mid * pen_prev_pieces)
        ob, lb = unit(q2b, k2c.reshape(2 * SPAN, LANES), v2c.reshape(2 * SPAN, LANES), 1, hp, None)
        o2_ref[:, :QUARTER, cols] = oa.reshape(SUB, QUARTER, LANES).astype(BF16)
        o2_ref[:, QUARTER:, cols] = ob.reshape(SUB, QUARTER, LANES).astype(BF16)
        l2a, l2b = put(l2a, la), put(l2b, lb)

        q3, k3, v3 = q3_ref[:, cols], k3_ref[:, cols], v3_ref[:, cols]
        oa, la = unit(q3[:SPAN], jnp.concatenate([k3[:SPAN], k3[:SPAN]], axis=0),
                      jnp.concatenate([v3[:SPAN], v3[:SPAN]], axis=0), 2, hp, pen_prev_half)
        ob, lb = unit(q3[SPAN:], k3, v3, 2, hp, None)
        o3_ref[:SPAN, cols] = oa.astype(BF16)
        o3_ref[SPAN:, cols] = ob.astype(BF16)
        l3a, l3b = put(l3a, la), put(l3b, lb)
        return l1a, l1b, l2a, l2b, l3a, l3b

    zero = jnp.zeros((SPAN, LANES), F32)
    l1a, l1b, l2a, l2b, l3a, l3b = lax.fori_loop(0, B_HEADS // 2, pair_body, (zero,) * 6)
    l1_ref[...] = jnp.concatenate(
        [l1a.reshape(RES, HALF, LANES), l1b.reshape(RES, HALF, LANES)], axis=1)
    l2_ref[:, :QUARTER] = l2a.reshape(SUB, QUARTER, LANES)
    l2_ref[:, QUARTER:] = l2b.reshape(SUB, QUARTER, LANES)
    l3_ref[:SPAN] = l3a
    l3_ref[SPAN:] = l3b


def _attention(qkv):
    batch, seq, n_qkv = qkv.shape
    width = B_HEADS * B_HEAD_DIM
    per = seq // RES
    n_steps = seq // (2 * SPAN)
    by_res = qkv.reshape(batch, RES, per, n_qkv)
    by_sub = qkv.reshape(batch, SUB, MID, per, n_qkv)

    def col(g, part):
        return g * 3 + part

    mid_steps = n_steps // MID

    def mid_spec(rows, part, prev):
        def index_map(b, s):
            blk = s % mid_steps
            if prev:
                blk = jnp.maximum(2 * blk - 1, 0)
            return (b, 0, s // mid_steps, blk, col(1, part))
        return pl.BlockSpec((None, SUB, None, rows, width), index_map)

    in_specs = [
        pl.BlockSpec((None, RES, 2 * HALF, width), lambda b, s: (b, 0, s, col(0, 0))),
        pl.BlockSpec((None, SPAN, width), lambda b, s: (b, jnp.maximum(2 * s - 1, 0), col(0, 1))),
        pl.BlockSpec((None, 2 * SPAN, width), lambda b, s: (b, s, col(0, 1))),
        pl.BlockSpec((None, SPAN, width), lambda b, s: (b, jnp.maximum(2 * s - 1, 0), col(0, 2))),
        pl.BlockSpec((None, 2 * SPAN, width), lambda b, s: (b, s, col(0, 2))),
        mid_spec(2 * QUARTER, 0, False),
        mid_spec(QUARTER, 1, True), mid_spec(2 * QUARTER, 1, False),
        mid_spec(QUARTER, 2, True), mid_spec(2 * QUARTER, 2, False),
        pl.BlockSpec((None, None, per, width), lambda b, s: (b, s, 0, col(2, 0))),
        pl.BlockSpec((None, None, per, width), lambda b, s: (b, s, 0, col(2, 1))),
        pl.BlockSpec((None, None, per, width), lambda b, s: (b, s, 0, col(2, 2))),
    ]
    operands = [by_res, qkv, qkv, qkv, qkv] + [by_sub] * 5 + [by_res] * 3

    out_specs, out_shapes = [], []

    def outs_for(w, dtype):
        return (
            [pl.BlockSpec((None, RES, 2 * HALF, w), lambda b, s: (b, 0, s, 0)),
             pl.BlockSpec((None, SUB, None, 2 * QUARTER, w),
                          lambda b, s: (b, 0, s // mid_steps, s % mid_steps, 0)),
             pl.BlockSpec((None, None, per, w), lambda b, s: (b, s, 0, 0))],
            [jax.ShapeDtypeStruct((batch, RES, per, w), dtype),
             jax.ShapeDtypeStruct((batch, SUB, MID, per, w), dtype),
             jax.ShapeDtypeStruct((batch, RES, per, w), dtype)])
    o_specs, o_shapes = outs_for(width, BF16)
    l_specs, l_shapes = outs_for(LANES, F32)
    for g in range(N_PAT):
        out_specs += [o_specs[g], l_specs[g]]
        out_shapes += [o_shapes[g], l_shapes[g]]

    outs = pl.pallas_call(
        _attn_kernel,
        grid=(batch, n_steps),
        in_specs=in_specs,
        out_specs=out_specs,
        out_shape=out_shapes,
        scratch_shapes=[pltpu.VMEM((N_PAT, B_HEADS, SPAN, 2 * SPAN), F32)],
        compiler_params=pltpu.CompilerParams(
            dimension_semantics=("arbitrary", "arbitrary"),
            vmem_limit_bytes=VMEM_LIMIT_BYTES),
        name="dilated_attn",
    )(*operands)
    o = [outs[2 * g].reshape(batch, seq, width) for g in range(N_PAT)]
    lse = [outs[2 * g + 1].reshape(batch, seq, LANES) for g in range(N_PAT)]
    return o, lse


def _attn_out_kernel(o0_ref, o1_ref, o2_ref, l0_ref, l1_ref, l2_ref, x_ref, mod_ref, w_out_ref,
                     ln_g_ref, ln_b_ref, out_ref):
    lses = [l0_ref[...], l1_ref[...], l2_ref[...]]
    mx = jnp.maximum(jnp.maximum(lses[0], lses[1]), lses[2])
    es = [jnp.exp(l - mx) for l in lses]
    den = es[0] + es[1] + es[2]
    src_lane = lax.broadcasted_iota(jnp.int32, (LANES, D_MODEL), 0)
    head_of_col = lax.broadcasted_iota(jnp.int32, (LANES, D_MODEL), 1) // B_HEAD_DIM
    expand = jnp.logical_and(src_lane < 2 * B_HEADS, src_lane % B_HEADS == head_of_col)
    expand = expand.astype(BF16)
    lane = lax.broadcasted_iota(jnp.int32, lses[0].shape, 1)
    o = None
    for e, o_ref in zip(es, (o0_ref, o1_ref, o2_ref)):
        wts = e / den
        head = wts.astype(BF16)
        rest = (wts - head.astype(F32)).astype(BF16)
        split = jnp.where(lane < B_HEADS, head, jnp.where(lane < 2 * B_HEADS, rest, 0.0))
        w = jnp.dot(split.astype(BF16), expand, preferred_element_type=F32)
        term = w * o_ref[...].astype(F32)
        o = term if o is None else o + term
    y = jnp.dot(o.astype(BF16), w_out_ref[...], preferred_element_type=F32)
    mod = mod_ref[...]
    out_ref[...] = _post_norm(x_ref[...], y, mod, ln_g_ref[...], ln_b_ref[...])


def _attn_out(o, lse, x, mod, w_out, ln_g, ln_b):
    batch, seq, _ = x.shape
    tm = 512
    row = lambda a: a.reshape(1, -1)
    tile = lambda w: pl.BlockSpec((None, tm, w), lambda b, i: (b, i, 0))
    return pl.pallas_call(
        _attn_out_kernel,
        grid=(batch, seq // tm),
        in_specs=[tile(D_MODEL)] * 3 + [tile(LANES)] * 3 + [
            tile(D_MODEL),
            pl.BlockSpec((None, 3, D_MODEL), lambda b, i: (b, 0, 0)),
            _resident((D_MODEL, D_MODEL)),
            _resident((1, D_MODEL)),
            _resident((1, D_MODEL)),
        ],
        out_specs=tile(D_MODEL),
        out_shape=jax.ShapeDtypeStruct(x.shape, F32),
        compiler_params=pltpu.CompilerParams(
            dimension_semantics=("arbitrary", "arbitrary"),
            vmem_limit_bytes=VMEM_LIMIT_BYTES),
        name="attn_out",
    )(*o, *lse, x, mod, w_out.astype(BF16), row(ln_g), row(ln_b))


def kernel(x, c, ada_w, ada_b, ln_g, ln_b, a_w_in, a_b_in, a_vn_g, a_vn_b, a_w_s, a_b_s, a_w_out,
           b_w_qkv, b_w_out, mlp_w_up, mlp_w_down):
    mods = _ada_mod(c, ada_w, ada_b)
    x = _mixer_a(x, mods[0], a_w_in[0], a_b_in[0], a_vn_g[0], a_vn_b[0], a_w_s[0], a_b_s[0],
                 a_w_out[0], ln_g[0, 0], ln_b[0, 0])
    x, x_res = _mlp(x, mods[1], mlp_w_up[0], mlp_w_down[0], ln_g[0, 1], ln_b[0, 1],
                    residue_in=False)
    qkv = _qkv(x, x_res, mods[2], b_w_qkv[0])
    o, lse = _attention(qkv)
    x_res = _attn_out(o, lse, x_res, mods[2], b_w_out[0], ln_g[1, 0], ln_b[1, 0])
    return _mlp(x_res, mods[3], mlp_w_up[1], mlp_w_down[1], ln_g[1, 1], ln_b[1, 1],
                residue_in=True)
```

```python
import functools

import jax
import jax.numpy as jnp
from jax import lax
from jax.experimental import pallas as pl
from jax.experimental.pallas import tpu as pltpu

D_MODEL = 1024
DEPTH = 2
CHUNK = 128
A_GROUPS = 16
A_GROUP_DIM = D_MODEL // A_GROUPS
B_HEADS = 16
B_HEAD_DIM = D_MODEL // B_HEADS
B_PATTERNS = ((128, 1), (512, 4), (2048, 16))
N_PAT = len(B_PATTERNS)
SPAN = 128
D_FF = 4 * D_MODEL
ALPHA = (2 * DEPTH) ** 0.25
LN_EPS = 1e-5
NEG = -1e30

LANES = 128
SUBLANES = 8
VMEM_LIMIT_BYTES = 56 * 1024 * 1024

RES = 16
MID = 4
SUB = RES // MID

F32 = jnp.float32
BF16 = jnp.bfloat16


def _layer_norm(x, g, b):
    mu = jnp.mean(x, axis=-1, keepdims=True)
    xc = x - mu
    var = jnp.mean(xc * xc, axis=-1, keepdims=True)
    return xc * lax.rsqrt(var + LN_EPS) * g + b


def _resident(shape):
    zeros = (0,) * len(shape)
    return pl.BlockSpec(shape, lambda *_: zeros, pipeline_mode=pl.Buffered(1))


def _ada_kernel(c_ref, w_ref, b_ref, o_ref):
    s = jax.nn.silu(c_ref[...])
    o_ref[0] = jnp.dot(s, w_ref[0], preferred_element_type=F32,
                       precision=lax.Precision.HIGHEST) + b_ref[0]


def _ada_mod(c, ada_w, ada_b):
    batch = c.shape[0]
    rows = SUBLANES
    c_pad = jnp.zeros((rows, D_MODEL), F32).at[:batch].set(c)
    n_sub = DEPTH * 2
    w = ada_w.reshape(n_sub, D_MODEL, 3 * D_MODEL)
    b = ada_b.reshape(n_sub, 1, 3 * D_MODEL)
    tn = 1024
    out = pl.pallas_call(
        _ada_kernel,
        grid=(n_sub, 3 * D_MODEL // tn),
        in_specs=[
            pl.BlockSpec((rows, D_MODEL), lambda i, j: (0, 0)),
            pl.BlockSpec((1, D_MODEL, tn), lambda i, j: (i, 0, j)),
            pl.BlockSpec((1, 1, tn), lambda i, j: (i, 0, j)),
        ],
        out_specs=pl.BlockSpec((1, rows, tn), lambda i, j: (i, 0, j)),
        out_shape=jax.ShapeDtypeStruct((n_sub, rows, 3 * D_MODEL), F32),
        name="ada_mod",
    )(c_pad, w, b)
    return out[:, :batch].reshape(n_sub, batch, 3, D_MODEL)


def _modulate(x, mod):
    shift = mod[0:1]
    scale = mod[1:2]
    return (x * (1.0 + scale) + shift).astype(BF16)


def _post_norm(x, y, mod, g, b):
    gate = 1.0 + mod[2:3]
    return _layer_norm(ALPHA * x + gate * y, g, b)


def _mixer_a_kernel(x_ref, mod_ref, w_in_ref, b_in_ref, vn_g_ref, vn_b_ref, ws_ref, bs_ref,
                    w_out_ref, ln_g_ref, ln_b_ref, o_ref, ws_scr, uv_scr, v_scr, g_scr):
    tm = x_ref.shape[0]

    @pl.when(jnp.logical_and(pl.program_id(0) == 0, pl.program_id(1) == 0))
    def _():
        t = lax.broadcasted_iota(jnp.int32, (CHUNK, CHUNK), 0)
        s = lax.broadcasted_iota(jnp.int32, (CHUNK, CHUNK), 1)
        causal = s <= t
        for g in range(A_GROUPS):
            ws_scr[g] = jnp.where(causal, ws_ref[g], 0.0).astype(BF16)

    x = x_ref[...]
    mod = mod_ref[...]
    h = _modulate(x, mod)
    uv = jnp.dot(h, w_in_ref[...], preferred_element_type=F32) + b_in_ref[...]
    uv_scr[...] = jax.nn.gelu(uv)
    v = _layer_norm(uv_scr[:, D_MODEL:], vn_g_ref[...], vn_b_ref[...])
    v_scr[...] = v.astype(BF16)

    low_half = lax.broadcasted_iota(jnp.int32, (CHUNK, LANES), 1) < A_GROUP_DIM

    def chunk_body(c, carry):
        r0 = pl.multiple_of(c * CHUNK, CHUNK)
        for gp in range(A_GROUPS // 2):
            cols = slice(gp * LANES, (gp + 1) * LANES)
            vb = v_scr[pl.ds(r0, CHUNK), cols]
            za = jnp.dot(ws_scr[2 * gp], vb, preferred_element_type=F32)
            zb = jnp.dot(ws_scr[2 * gp + 1], vb, preferred_element_type=F32)
            z = jnp.where(low_half, za, zb) + bs_ref[:, cols]
            u = uv_scr[pl.ds(r0, CHUNK), cols]
            g_scr[pl.ds(r0, CHUNK), cols] = (u * z).astype(BF16)
        return carry

    lax.fori_loop(0, tm // CHUNK, chunk_body, 0)

    y = jnp.dot(g_scr[...], w_out_ref[...], preferred_element_type=F32)
    o_ref[...] = _post_norm(x, y, mod, ln_g_ref[...], ln_b_ref[...])


def _mixer_a(x, mod, w_in, b_in, vn_g, vn_b, w_s, b_s, w_out, ln_g, ln_b):
    batch, seq, _ = x.shape
    tm = 512
    row = lambda a: a.reshape(1, -1)
    bs_plane = jnp.repeat(b_s.T, A_GROUP_DIM, axis=1)
    return pl.pallas_call(
        _mixer_a_kernel,
        grid=(batch, seq // tm),
        in_specs=[
            pl.BlockSpec((None, tm, D_MODEL), lambda b, i: (b, i, 0)),
            pl.BlockSpec((None, 3, D_MODEL), lambda b, i: (b, 0, 0)),
            _resident((D_MODEL, 2 * D_MODEL)),
            _resident((1, 2 * D_MODEL)),
            _resident((1, D_MODEL)),
            _resident((1, D_MODEL)),
            _resident((A_GROUPS, CHUNK, CHUNK)),
            _resident((CHUNK, D_MODEL)),
            _resident((D_MODEL, D_MODEL)),
            _resident((1, D_MODEL)),
            _resident((1, D_MODEL)),
        ],
        out_specs=pl.BlockSpec((None, tm, D_MODEL), lambda b, i: (b, i, 0)),
        out_shape=jax.ShapeDtypeStruct(x.shape, F32),
        scratch_shapes=[
            pltpu.VMEM((A_GROUPS, CHUNK, CHUNK), BF16),
            pltpu.VMEM((tm, 2 * D_MODEL), F32),
            pltpu.VMEM((tm, D_MODEL), BF16),
            pltpu.VMEM((tm, D_MODEL), BF16),
        ],
        compiler_params=pltpu.CompilerParams(
            dimension_semantics=("arbitrary", "arbitrary"),
            vmem_limit_bytes=VMEM_LIMIT_BYTES),
        name="mixer_a",
    )(x, mod, w_in.astype(BF16), row(b_in), row(vn_g), row(vn_b), w_s, bs_plane,
      w_out.astype(BF16), row(ln_g), row(ln_b))


def _mlp_kernel(residue_in, x_ref, mod_ref, w_up_ref, w_down_ref, ln_g_ref, ln_b_ref, *o_refs):
    tm = o_refs[0].shape[0]
    x = x_ref[...].reshape(tm, D_MODEL)
    mod = mod_ref[...]
    h = _modulate(x, mod)
    a = jnp.dot(h, w_up_ref[...], preferred_element_type=F32)
    r = jnp.square(jnp.maximum(a, 0.0)).astype(BF16)
    y = jnp.dot(r, w_down_ref[...], preferred_element_type=F32)
    out = _post_norm(x, y, mod, ln_g_ref[...], ln_b_ref[...])
    if residue_in:
        (nat_ref,) = o_refs
        nat_ref[...] = jnp.swapaxes(out.reshape(RES, tm // RES, D_MODEL), 0, 1).reshape(tm, D_MODEL)
    else:
        nat_ref, res_ref = o_refs
        nat_ref[...] = out
        res_ref[...] = jnp.swapaxes(out.reshape(tm // RES, RES, D_MODEL), 0, 1)


def _mlp(x, mod, w_up, w_down, ln_g, ln_b, residue_in):
    batch, seq, _ = x.shape
    tm = 512
    row = lambda a: a.reshape(1, -1)
    nat_spec = pl.BlockSpec((None, tm, D_MODEL), lambda b, i: (b, i, 0))
    res_spec = pl.BlockSpec((None, RES, tm // RES, D_MODEL), lambda b, i: (b, 0, i, 0))
    nat_shape = jax.ShapeDtypeStruct((batch, seq, D_MODEL), F32)
    res_shape = jax.ShapeDtypeStruct((batch, RES, seq // RES, D_MODEL), F32)
    if residue_in:
        x = x.reshape(res_shape.shape)
    outs = pl.pallas_call(
        functools.partial(_mlp_kernel, residue_in),
        grid=(batch, seq // tm),
        in_specs=[
            res_spec if residue_in else nat_spec,
            pl.BlockSpec((None, 3, D_MODEL), lambda b, i: (b, 0, 0)),
            _resident((D_MODEL, D_FF)),
            _resident((D_FF, D_MODEL)),
            _resident((1, D_MODEL)),
            _resident((1, D_MODEL)),
        ],
        out_specs=[nat_spec] if residue_in else [nat_spec, res_spec],
        out_shape=[nat_shape] if residue_in else [nat_shape, res_shape],
        compiler_params=pltpu.CompilerParams(
            dimension_semantics=("arbitrary", "arbitrary"),
            vmem_limit_bytes=VMEM_LIMIT_BYTES),
        name="mlp",
    )(x, mod, w_up.astype(BF16), w_down.astype(BF16), row(ln_g), row(ln_b))
    if residue_in:
        return outs[0]
    return outs[0], outs[1].reshape(batch, seq, D_MODEL)


LOCAL_KV_BLOCKS = (1, 2)


def _qkv_kernel(x_nat_ref, x_res_ref, mod_ref, w_ref, o_ref):
    mod = mod_ref[...]
    h_nat = _modulate(x_nat_ref[...], mod)
    h_res = _modulate(x_res_ref[...], mod)
    width = B_HEADS * B_HEAD_DIM
    for c in range(o_ref.shape[1] // width):
        cols = slice(c * width, (c + 1) * width)
        h = h_nat if c in LOCAL_KV_BLOCKS else h_res
        o_ref[:, cols] = jnp.dot(h, w_ref[:, cols], preferred_element_type=F32).astype(BF16)


def _qkv(x_nat, x_res, mod, w_qkv):
    batch, seq, _ = x_nat.shape
    n_out = w_qkv.shape[1]
    tm = 512
    return pl.pallas_call(
        _qkv_kernel,
        grid=(batch, seq // tm),
        in_specs=[
            pl.BlockSpec((None, tm, D_MODEL), lambda b, i: (b, i, 0)),
            pl.BlockSpec((None, tm, D_MODEL), lambda b, i: (b, i, 0)),
            pl.BlockSpec((None, 3, D_MODEL), lambda b, i: (b, 0, 0)),
            _resident((D_MODEL, n_out)),
        ],
        out_specs=pl.BlockSpec((None, tm, n_out), lambda b, i: (b, i, 0)),
        out_shape=jax.ShapeDtypeStruct((batch, seq, n_out), BF16),
        compiler_params=pltpu.CompilerParams(
            dimension_semantics=("arbitrary", "arbitrary"),
            vmem_limit_bytes=VMEM_LIMIT_BYTES),
        name="qkv_proj",
    )(x_nat, x_res, mod, w_qkv.astype(BF16))


HALF = SPAN // RES
QUARTER = SPAN // SUB
PAIR_UNROLL = 8


def _build_bias_tables(bias_scr):
    row = lax.broadcasted_iota(jnp.int32, (SPAN, 2 * SPAN), 0)
    col = lax.broadcasted_iota(jnp.int32, (SPAN, 2 * SPAN), 1)
    diff_local = SPAN + RES * (row % HALF) + row // HALF - col
    diff_mid = (SUB * (row % QUARTER) + row // QUARTER
                - SUB * (col % (2 * QUARTER) - QUARTER) - col // (2 * QUARTER))
    diff_wide = SPAN + row - col
    has_prev = (None, None, None, col >= SPAN, col % (2 * QUARTER) >= QUARTER)
    for t, diff in enumerate((diff_local, diff_mid, diff_wide, diff_local, diff_mid)):
        dilation = B_PATTERNS[t % N_PAT][1]
        valid = jnp.logical_and(diff >= 0, diff <= SPAN)
        if has_prev[t] is not None:
            valid = jnp.logical_and(valid, has_prev[t])
        dist = (dilation * diff).astype(F32)
        for hd in range(B_HEADS):
            slope = 2.0 ** (-8.0 * (hd + 1) / B_HEADS)
            bias_scr[t, hd] = jnp.where(valid, -slope * dist, NEG)


def _attn_kernel(q1_ref, k1p_ref, k1c_ref, v1p_ref, v1c_ref,
                 q2_ref, k2p_ref, k2c_ref, v2p_ref, v2c_ref,
                 q3_ref, k3_ref, v3_ref,
                 o1_ref, l1_ref, o2_ref, l2_ref, o3_ref, l3_ref, bias_scr):
    step = pl.program_id(1)

    @pl.when(jnp.logical_and(pl.program_id(0) == 0, step == 0))
    def _():
        _build_bias_tables(bias_scr)

    lane = lax.broadcasted_iota(jnp.int32, (SPAN, LANES), 1)
    low_half = lane < B_HEAD_DIM
    lane_head = lane % B_HEADS
    scale = B_HEAD_DIM ** -0.5
    q_mask = (jnp.where(low_half, scale, 0.0).astype(BF16),
              jnp.where(low_half, 0.0, scale).astype(BF16))

    mid_steps = pl.num_programs(1) // MID
    table_local_a = jnp.where(step == 0, N_PAT, 0)
    table_mid_a = jnp.where(step % mid_steps == 0, N_PAT + 1, 1)

    def unit(q2, k2, v2, table, hp, own_block_only=False):
        q_both = jnp.concatenate([q2 * q_mask[0], q2 * q_mask[1]], axis=0)
        s_both = lax.dot_general(q_both, k2, (((1,), (1,)), ((), ())),
                                 preferred_element_type=F32)
        ps, ms, ls = [], [], []
        for j in range(2):
            hd = 2 * hp + j
            s = s_both[j * SPAN:(j + 1) * SPAN]
            if own_block_only:
                s = s + bias_scr[table, hd, :, SPAN:]
            else:
                s = s + bias_scr[table, hd]
            m = jnp.max(s, axis=-1, keepdims=True)
            p = jnp.exp(s - m)
            ls.append(jnp.sum(p, axis=-1, keepdims=True))
            ms.append(m)
            ps.append(p.astype(BF16))
        pv = jnp.dot(jnp.concatenate(ps, axis=0), v2, preferred_element_type=F32)
        o = jnp.where(low_half, pv[:SPAN] / ls[0], pv[SPAN:] / ls[1])
        lses = [ms[j] + jnp.log(ls[j]) for j in range(2)]
        return o, lses

    def pair_body(hp, carry):
        l1a, l1b, l2a, l2b, l3a, l3b = carry
        cols = pl.ds(pl.multiple_of(hp * LANES, LANES), LANES)

        def put(tile, lses):
            for j in range(2):
                tile = jnp.where(lane_head == 2 * hp + j, lses[j], tile)
            return tile

        q1 = q1_ref[:, :, cols].astype(F32).reshape(RES, 2, HALF, LANES)
        q1a = q1[:, 0].reshape(SPAN, LANES).astype(BF16)
        q1b = q1[:, 1].reshape(SPAN, LANES).astype(BF16)
        k1c = k1c_ref[:, cols]
        v1c = v1c_ref[:, cols]
        oa, la = unit(q1a, jnp.concatenate([k1p_ref[:, cols], k1c[:SPAN]], axis=0),
                      jnp.concatenate([v1p_ref[:, cols], v1c[:SPAN]], axis=0),
                      table_local_a, hp)
        ob, lb = unit(q1b, k1c, v1c, 0, hp)
        o1 = jnp.concatenate([oa.reshape(RES, HALF, LANES), ob.reshape(RES, HALF, LANES)], axis=1)
        o1_ref[:, :, cols] = o1.astype(BF16)
        l1a, l1b = put(l1a, la), put(l1b, lb)

        q2 = q2_ref[:, :, cols]
        k2p, k2c = k2p_ref[:, :, cols], k2c_ref[:, :, cols]
        v2p, v2c = v2p_ref[:, :, cols], v2c_ref[:, :, cols]
        q2a = q2[:, :QUARTER].reshape(SPAN, LANES)
        q2b = q2[:, QUARTER:].reshape(SPAN, LANES)
        k2a = jnp.concatenate([k2p, k2c[:, :QUARTER]], axis=1).reshape(2 * SPAN, LANES)
        v2a = jnp.concatenate([v2p, v2c[:, :QUARTER]], axis=1).reshape(2 * SPAN, LANES)
        oa, la = unit(q2a, k2a, v2a, table_mid_a, hp)
        ob, lb = unit(q2b, k2c.reshape(2 * SPAN, LANES), v2c.reshape(2 * SPAN, LANES), 1, hp)
        o2_ref[:, :QUARTER, cols] = oa.reshape(SUB, QUARTER, LANES).astype(BF16)
        o2_ref[:, QUARTER:, cols] = ob.reshape(SUB, QUARTER, LANES).astype(BF16)
        l2a, l2b = put(l2a, la), put(l2b, lb)

        q3, k3, v3 = q3_ref[:, cols], k3_ref[:, cols], v3_ref[:, cols]
        oa, la = unit(q3[:SPAN], k3[:SPAN], v3[:SPAN], 2, hp, own_block_only=True)
        ob, lb = unit(q3[SPAN:], k3, v3, 2, hp)
        o3_ref[:SPAN, cols] = oa.astype(BF16)
        o3_ref[SPAN:, cols] = ob.astype(BF16)
        l3a, l3b = put(l3a, la), put(l3b, lb)
        return l1a, l1b, l2a, l2b, l3a, l3b

    zero = jnp.zeros((SPAN, LANES), F32)
    l1a, l1b, l2a, l2b, l3a, l3b = lax.fori_loop(0, B_HEADS // 2, pair_body, (zero,) * 6,
                                                 unroll=PAIR_UNROLL)
    l1_ref[...] = jnp.concatenate(
        [l1a.reshape(RES, HALF, LANES), l1b.reshape(RES, HALF, LANES)], axis=1)
    l2_ref[:, :QUARTER] = l2a.reshape(SUB, QUARTER, LANES)
    l2_ref[:, QUARTER:] = l2b.reshape(SUB, QUARTER, LANES)
    l3_ref[:SPAN] = l3a
    l3_ref[SPAN:] = l3b


def _attention(qkv):
    batch, seq, n_qkv = qkv.shape
    width = B_HEADS * B_HEAD_DIM
    per = seq // RES
    n_steps = seq // (2 * SPAN)
    by_res = qkv.reshape(batch, RES, per, n_qkv)
    by_sub = qkv.reshape(batch, SUB, MID, per, n_qkv)

    def col(g, part):
        return g * 3 + part

    mid_steps = n_steps // MID

    def mid_spec(rows, part, prev):
        def index_map(b, s):
            blk = s % mid_steps
            if prev:
                blk = jnp.maximum(2 * blk - 1, 0)
            return (b, 0, s // mid_steps, blk, col(1, part))
        return pl.BlockSpec((None, SUB, None, rows, width), index_map)

    in_specs = [
        pl.BlockSpec((None, RES, 2 * HALF, width), lambda b, s: (b, 0, s, col(0, 0))),
        pl.BlockSpec((None, SPAN, width), lambda b, s: (b, jnp.maximum(2 * s - 1, 0), col(0, 1))),
        pl.BlockSpec((None, 2 * SPAN, width), lambda b, s: (b, s, col(0, 1))),
        pl.BlockSpec((None, SPAN, width), lambda b, s: (b, jnp.maximum(2 * s - 1, 0), col(0, 2))),
        pl.BlockSpec((None, 2 * SPAN, width), lambda b, s: (b, s, col(0, 2))),
        mid_spec(2 * QUARTER, 0, False),
        mid_spec(QUARTER, 1, True), mid_spec(2 * QUARTER, 1, False),
        mid_spec(QUARTER, 2, True), mid_spec(2 * QUARTER, 2, False),
        pl.BlockSpec((None, None, per, width), lambda b, s: (b, s, 0, col(2, 0))),
        pl.BlockSpec((None, None, per, width), lambda b, s: (b, s, 0, col(2, 1))),
        pl.BlockSpec((None, None, per, width), lambda b, s: (b, s, 0, col(2, 2))),
    ]
    operands = [by_res, qkv, qkv, qkv, qkv] + [by_sub] * 5 + [by_res] * 3

    out_specs, out_shapes = [], []

    def outs_for(w, dtype):
        return (
            [pl.BlockSpec((None, RES, 2 * HALF, w), lambda b, s: (b, 0, s, 0)),
             pl.BlockSpec((None, SUB, None, 2 * QUARTER, w),
                          lambda b, s: (b, 0, s // mid_steps, s % mid_steps, 0)),
             pl.BlockSpec((None, None, per, w), lambda b, s: (b, s, 0, 0))],
            [jax.ShapeDtypeStruct((batch, RES, per, w), dtype),
             jax.ShapeDtypeStruct((batch, SUB, MID, per, w), dtype),
             jax.ShapeDtypeStruct((batch, RES, per, w), dtype)])
    o_specs, o_shapes = outs_for(width, BF16)
    l_specs, l_shapes = outs_for(LANES, F32)
    for g in range(N_PAT):
        out_specs += [o_specs[g], l_specs[g]]
        out_shapes += [o_shapes[g], l_shapes[g]]

    outs = pl.pallas_call(
        _attn_kernel,
        grid=(batch, n_steps),
        in_specs=in_specs,
        out_specs=out_specs,
        out_shape=out_shapes,
        scratch_shapes=[pltpu.VMEM((N_PAT + 2, B_HEADS, SPAN, 2 * SPAN), F32)],
        compiler_params=pltpu.CompilerParams(
            dimension_semantics=("arbitrary", "arbitrary"),
            vmem_limit_bytes=VMEM_LIMIT_BYTES),
        name="dilated_attn",
    )(*operands)
    o = [outs[2 * g].reshape(batch, seq, width) for g in range(N_PAT)]
    lse = [outs[2 * g + 1].reshape(batch, seq, LANES) for g in range(N_PAT)]
    return o, lse


def _attn_out_kernel(o0_ref, o1_ref, o2_ref, l0_ref, l1_ref, l2_ref, x_ref, mod_ref, w_out_ref,
                     ln_g_ref, ln_b_ref, out_ref):
    lses = [l0_ref[...], l1_ref[...], l2_ref[...]]
    mx = jnp.maximum(jnp.maximum(lses[0], lses[1]), lses[2])
    es = [jnp.exp(l - mx) for l in lses]
    den = es[0] + es[1] + es[2]
    src_lane = lax.broadcasted_iota(jnp.int32, (LANES, D_MODEL), 0)
    head_of_col = lax.broadcasted_iota(jnp.int32, (LANES, D_MODEL), 1) // B_HEAD_DIM
    expand = jnp.logical_and(src_lane < 2 * B_HEADS, src_lane % B_HEADS == head_of_col)
    expand = expand.astype(BF16)
    lane = lax.broadcasted_iota(jnp.int32, lses[0].shape, 1)
    o = None
    for e, o_ref in zip(es, (o0_ref, o1_ref, o2_ref)):
        wts = e / den
        head = wts.astype(BF16)
        rest = (wts - head.astype(F32)).astype(BF16)
        split = jnp.where(lane < B_HEADS, head, jnp.where(lane < 2 * B_HEADS, rest, 0.0))
        w = jnp.dot(split.astype(BF16), expand, preferred_element_type=F32)
        term = w * o_ref[...].astype(F32)
        o = term if o is None else o + term
    y = jnp.dot(o.astype(BF16), w_out_ref[...], preferred_element_type=F32)
    mod = mod_ref[...]
    out_ref[...] = _post_norm(x_ref[...], y, mod, ln_g_ref[...], ln_b_ref[...])


def _attn_out(o, lse, x, mod, w_out, ln_g, ln_b):
    batch, seq, _ = x.shape
    tm = 512
    row = lambda a: a.reshape(1, -1)
    tile = lambda w: pl.BlockSpec((None, tm, w), lambda b, i: (b, i, 0))
    return pl.pallas_call(
        _attn_out_kernel,
        grid=(batch, seq // tm),
        in_specs=[tile(D_MODEL)] * 3 + [tile(LANES)] * 3 + [
            tile(D_MODEL),
            pl.BlockSpec((None, 3, D_MODEL), lambda b, i: (b, 0, 0)),
            _resident((D_MODEL, D_MODEL)),
            _resident((1, D_MODEL)),
            _resident((1, D_MODEL)),
        ],
        out_specs=tile(D_MODEL),
        out_shape=jax.ShapeDtypeStruct(x.shape, F32),
        compiler_params=pltpu.CompilerParams(
            dimension_semantics=("arbitrary", "arbitrary"),
            vmem_limit_bytes=VMEM_LIMIT_BYTES),
        name="attn_out",
    )(*o, *lse, x, mod, w_out.astype(BF16), row(ln_g), row(ln_b))


def kernel(x, c, ada_w, ada_b, ln_g, ln_b, a_w_in, a_b_in, a_vn_g, a_vn_b, a_w_s, a_b_s, a_w_out,
           b_w_qkv, b_w_out, mlp_w_up, mlp_w_down):
    mods = _ada_mod(c, ada_w, ada_b)
    x = _mixer_a(x, mods[0], a_w_in[0], a_b_in[0], a_vn_g[0], a_vn_b[0], a_w_s[0], a_b_s[0],
                 a_w_out[0], ln_g[0, 0], ln_b[0, 0])
    x, x_res = _mlp(x, mods[1], mlp_w_up[0], mlp_w_down[0], ln_g[0, 1], ln_b[0, 1],
                    residue_in=False)
    qkv = _qkv(x, x_res, mods[2], b_w_qkv[0])
    o, lse = _attention(qkv)
    x_res = _attn_out(o, lse, x_res, mods[2], b_w_out[0], ln_g[1, 0], ln_b[1, 0])
    return _mlp(x_res, mods[3], mlp_w_up[1], mlp_w_down[1], ln_g[1, 1], ln_b[1, 1],
                residue_in=True)
```

```python
import functools

import jax
import jax.numpy as jnp
from jax import lax
from jax.experimental import pallas as pl
from jax.experimental.pallas import tpu as pltpu

D_MODEL = 1024
DEPTH = 2
CHUNK = 128
A_GROUPS = 16
A_GROUP_DIM = D_MODEL // A_GROUPS
B_HEADS = 16
B_HEAD_DIM = D_MODEL // B_HEADS
B_PATTERNS = ((128, 1), (512, 4), (2048, 16))
N_PAT = len(B_PATTERNS)
SPAN = 128
D_FF = 4 * D_MODEL
ALPHA = (2 * DEPTH) ** 0.25
LN_EPS = 1e-5
NEG = -1e30

LANES = 128
SUBLANES = 8
VMEM_LIMIT_BYTES = 56 * 1024 * 1024

RES = 16
MID = 4
SUB = RES // MID

F32 = jnp.float32
BF16 = jnp.bfloat16


def _layer_norm(x, g, b):
    mu = jnp.mean(x, axis=-1, keepdims=True)
    xc = x - mu
    var = jnp.mean(xc * xc, axis=-1, keepdims=True)
    return xc * lax.rsqrt(var + LN_EPS) * g + b


def _resident(shape):
    zeros = (0,) * len(shape)
    return pl.BlockSpec(shape, lambda *_: zeros, pipeline_mode=pl.Buffered(1))


def _ada_kernel(c_ref, w_ref, b_ref, o_ref):
    s = jax.nn.silu(c_ref[...]).astype(BF16)
    o_ref[0] = jnp.dot(s, w_ref[0].astype(BF16), preferred_element_type=F32) + b_ref[0]


def _ada_mod(c, ada_w, ada_b):
    batch = c.shape[0]
    rows = SUBLANES
    c_pad = jnp.zeros((rows, D_MODEL), F32).at[:batch].set(c)
    n_sub = DEPTH * 2
    w = ada_w.reshape(n_sub, D_MODEL, 3 * D_MODEL)
    b = ada_b.reshape(n_sub, 1, 3 * D_MODEL)
    tn = 1024
    out = pl.pallas_call(
        _ada_kernel,
        grid=(n_sub, 3 * D_MODEL // tn),
        in_specs=[
            pl.BlockSpec((rows, D_MODEL), lambda i, j: (0, 0)),
            pl.BlockSpec((1, D_MODEL, tn), lambda i, j: (i, 0, j)),
            pl.BlockSpec((1, 1, tn), lambda i, j: (i, 0, j)),
        ],
        out_specs=pl.BlockSpec((1, rows, tn), lambda i, j: (i, 0, j)),
        out_shape=jax.ShapeDtypeStruct((n_sub, rows, 3 * D_MODEL), F32),
        name="ada_mod",
    )(c_pad, w, b)
    return out[:, :batch].reshape(n_sub, batch, 3, D_MODEL)


def _modulate(x, mod):
    shift = mod[0:1]
    scale = mod[1:2]
    return (x * (1.0 + scale) + shift).astype(BF16)


def _post_norm(x, y, mod, g, b):
    gate = 1.0 + mod[2:3]
    return _layer_norm(ALPHA * x + gate * y, g, b)


def _mixer_a_kernel(x_ref, mod_ref, w_in_ref, b_in_ref, vn_g_ref, vn_b_ref, ws_ref, bs_ref,
                    w_out_ref, ln_g_ref, ln_b_ref, o_ref, ws_scr, uv_scr, v_scr, g_scr):
    tm = x_ref.shape[0]

    @pl.when(jnp.logical_and(pl.program_id(0) == 0, pl.program_id(1) == 0))
    def _():
        t = lax.broadcasted_iota(jnp.int32, (CHUNK, CHUNK), 0)
        s = lax.broadcasted_iota(jnp.int32, (CHUNK, CHUNK), 1)
        causal = s <= t
        for g in range(A_GROUPS):
            ws_scr[g] = jnp.where(causal, ws_ref[g], 0.0).astype(BF16)

    x = x_ref[...]
    mod = mod_ref[...]
    h = _modulate(x, mod)
    uv = jnp.dot(h, w_in_ref[...], preferred_element_type=F32) + b_in_ref[...]
    uv_scr[...] = jax.nn.gelu(uv)
    v = _layer_norm(uv_scr[:, D_MODEL:], vn_g_ref[...], vn_b_ref[...])
    v_scr[...] = v.astype(BF16)

    low_half = lax.broadcasted_iota(jnp.int32, (CHUNK, LANES), 1) < A_GROUP_DIM

    def chunk_body(c, carry):
        r0 = pl.multiple_of(c * CHUNK, CHUNK)
        for gp in range(A_GROUPS // 2):
            cols = slice(gp * LANES, (gp + 1) * LANES)
            vb = v_scr[pl.ds(r0, CHUNK), cols]
            za = jnp.dot(ws_scr[2 * gp], vb, preferred_element_type=F32)
            zb = jnp.dot(ws_scr[2 * gp + 1], vb, preferred_element_type=F32)
            z = jnp.where(low_half, za, zb) + bs_ref[:, cols]
            u = uv_scr[pl.ds(r0, CHUNK), cols]
            g_scr[pl.ds(r0, CHUNK), cols] = (u * z).astype(BF16)
        return carry

    lax.fori_loop(0, tm // CHUNK, chunk_body, 0, unroll=True)

    y = jnp.dot(g_scr[...], w_out_ref[...], preferred_element_type=F32)
    o_ref[...] = _post_norm(x, y, mod, ln_g_ref[...], ln_b_ref[...])


def _mixer_a(x, mod, w_in, b_in, vn_g, vn_b, w_s, b_s, w_out, ln_g, ln_b):
    batch, seq, _ = x.shape
    tm = 512
    row = lambda a: a.reshape(1, -1)
    bs_plane = jnp.repeat(b_s.T, A_GROUP_DIM, axis=1)
    return pl.pallas_call(
        _mixer_a_kernel,
        grid=(batch, seq // tm),
        in_specs=[
            pl.BlockSpec((None, tm, D_MODEL), lambda b, i: (b, i, 0)),
            pl.BlockSpec((None, 3, D_MODEL), lambda b, i: (b, 0, 0)),
            _resident((D_MODEL, 2 * D_MODEL)),
            _resident((1, 2 * D_MODEL)),
            _resident((1, D_MODEL)),
            _resident((1, D_MODEL)),
            _resident((A_GROUPS, CHUNK, CHUNK)),
            _resident((CHUNK, D_MODEL)),
            _resident((D_MODEL, D_MODEL)),
            _resident((1, D_MODEL)),
            _resident((1, D_MODEL)),
        ],
        out_specs=pl.BlockSpec((None, tm, D_MODEL), lambda b, i: (b, i, 0)),
        out_shape=jax.ShapeDtypeStruct(x.shape, F32),
        scratch_shapes=[
            pltpu.VMEM((A_GROUPS, CHUNK, CHUNK), BF16),
            pltpu.VMEM((tm, 2 * D_MODEL), F32),
            pltpu.VMEM((tm, D_MODEL), BF16),
            pltpu.VMEM((tm, D_MODEL), BF16),
        ],
        compiler_params=pltpu.CompilerParams(
            dimension_semantics=("arbitrary", "arbitrary"),
            vmem_limit_bytes=VMEM_LIMIT_BYTES),
        name="mixer_a",
    )(x, mod, w_in.astype(BF16), row(b_in), row(vn_g), row(vn_b), w_s, bs_plane,
      w_out.astype(BF16), row(ln_g), row(ln_b))


def _mlp_kernel(residue_in, x_ref, mod_ref, w_up_ref, w_down_ref, ln_g_ref, ln_b_ref, *o_refs):
    tm = o_refs[0].shape[0]
    x = x_ref[...].reshape(tm, D_MODEL)
    mod = mod_ref[...]
    h = _modulate(x, mod)
    a = jnp.dot(h, w_up_ref[...], preferred_element_type=F32)
    r = jnp.square(jnp.maximum(a, 0.0)).astype(BF16)
    y = jnp.dot(r, w_down_ref[...], preferred_element_type=F32)
    out = _post_norm(x, y, mod, ln_g_ref[...], ln_b_ref[...])
    if residue_in:
        (nat_ref,) = o_refs
        nat_ref[...] = jnp.swapaxes(out.reshape(RES, tm // RES, D_MODEL), 0, 1).reshape(tm, D_MODEL)
    else:
        nat_ref, res_ref = o_refs
        nat_ref[...] = out
        res_ref[...] = jnp.swapaxes(out.reshape(tm // RES, RES, D_MODEL), 0, 1)


def _mlp(x, mod, w_up, w_down, ln_g, ln_b, residue_in):
    batch, seq, _ = x.shape
    tm = 512
    row = lambda a: a.reshape(1, -1)
    nat_spec = pl.BlockSpec((None, tm, D_MODEL), lambda b, i: (b, i, 0))
    res_spec = pl.BlockSpec((None, RES, tm // RES, D_MODEL), lambda b, i: (b, 0, i, 0))
    nat_shape = jax.ShapeDtypeStruct((batch, seq, D_MODEL), F32)
    res_shape = jax.ShapeDtypeStruct((batch, RES, seq // RES, D_MODEL), F32)
    if residue_in:
        x = x.reshape(res_shape.shape)
    outs = pl.pallas_call(
        functools.partial(_mlp_kernel, residue_in),
        grid=(batch, seq // tm),
        in_specs=[
            res_spec if residue_in else nat_spec,
            pl.BlockSpec((None, 3, D_MODEL), lambda b, i: (b, 0, 0)),
            _resident((D_MODEL, D_FF)),
            _resident((D_FF, D_MODEL)),
            _resident((1, D_MODEL)),
            _resident((1, D_MODEL)),
        ],
        out_specs=[nat_spec] if residue_in else [nat_spec, res_spec],
        out_shape=[nat_shape] if residue_in else [nat_shape, res_shape],
        compiler_params=pltpu.CompilerParams(
            dimension_semantics=("arbitrary", "arbitrary"),
            vmem_limit_bytes=VMEM_LIMIT_BYTES),
        name="mlp",
    )(x, mod, w_up.astype(BF16), w_down.astype(BF16), row(ln_g), row(ln_b))
    if residue_in:
        return outs[0]
    return outs[0], outs[1].reshape(batch, seq, D_MODEL)


LOCAL_KV_BLOCKS = (1, 2)


def _qkv_kernel(x_nat_ref, x_res_ref, mod_ref, w_ref, o_ref):
    mod = mod_ref[...]
    h_nat = _modulate(x_nat_ref[...], mod)
    h_res = _modulate(x_res_ref[...], mod)
    width = B_HEADS * B_HEAD_DIM
    for c in range(o_ref.shape[1] // width):
        cols = slice(c * width, (c + 1) * width)
        h = h_nat if c in LOCAL_KV_BLOCKS else h_res
        o_ref[:, cols] = jnp.dot(h, w_ref[:, cols], preferred_element_type=F32).astype(BF16)


def _qkv(x_nat, x_res, mod, w_qkv):
    batch, seq, _ = x_nat.shape
    n_out = w_qkv.shape[1]
    tm = 512
    return pl.pallas_call(
        _qkv_kernel,
        grid=(batch, seq // tm),
        in_specs=[
            pl.BlockSpec((None, tm, D_MODEL), lambda b, i: (b, i, 0)),
            pl.BlockSpec((None, tm, D_MODEL), lambda b, i: (b, i, 0)),
            pl.BlockSpec((None, 3, D_MODEL), lambda b, i: (b, 0, 0)),
            _resident((D_MODEL, n_out)),
        ],
        out_specs=pl.BlockSpec((None, tm, n_out), lambda b, i: (b, i, 0)),
        out_shape=jax.ShapeDtypeStruct((batch, seq, n_out), BF16),
        compiler_params=pltpu.CompilerParams(
            dimension_semantics=("arbitrary", "arbitrary"),
            vmem_limit_bytes=VMEM_LIMIT_BYTES),
        name="qkv_proj",
    )(x_nat, x_res, mod, w_qkv.astype(BF16))


HALF = SPAN // RES
QUARTER = SPAN // SUB
PAIR_UNROLL = 8


def _build_bias_tables(bias_scr):
    row = lax.broadcasted_iota(jnp.int32, (SPAN, 2 * SPAN), 0)
    col = lax.broadcasted_iota(jnp.int32, (SPAN, 2 * SPAN), 1)
    diff_local = SPAN + RES * (row % HALF) + row // HALF - col
    diff_mid = (SUB * (row % QUARTER) + row // QUARTER
                - SUB * (col % (2 * QUARTER) - QUARTER) - col // (2 * QUARTER))
    diff_wide = SPAN + row - col
    has_prev = (None, None, None, col >= SPAN, col % (2 * QUARTER) >= QUARTER)
    for t, diff in enumerate((diff_local, diff_mid, diff_wide, diff_local, diff_mid)):
        dilation = B_PATTERNS[t % N_PAT][1]
        valid = jnp.logical_and(diff >= 0, diff <= SPAN)
        if has_prev[t] is not None:
            valid = jnp.logical_and(valid, has_prev[t])
        dist = (dilation * diff).astype(F32)
        for hd in range(B_HEADS):
            slope = 2.0 ** (-8.0 * (hd + 1) / B_HEADS)
            bias_scr[t, hd] = jnp.where(valid, -slope * dist, NEG)


def _attn_kernel(q1_ref, k1p_ref, k1c_ref, v1p_ref, v1c_ref,
                 q2_ref, k2p_ref, k2c_ref, v2p_ref, v2c_ref,
                 q3_ref, k3_ref, v3_ref,
                 o1_ref, l1_ref, o2_ref, l2_ref, o3_ref, l3_ref, bias_scr):
    step = pl.program_id(1)

    @pl.when(jnp.logical_and(pl.program_id(0) == 0, step == 0))
    def _():
        _build_bias_tables(bias_scr)

    lane = lax.broadcasted_iota(jnp.int32, (SPAN, LANES), 1)
    low_half = lane < B_HEAD_DIM
    lane_head = lane % B_HEADS
    scale = B_HEAD_DIM ** -0.5
    q_mask = (jnp.where(low_half, scale, 0.0).astype(BF16),
              jnp.where(low_half, 0.0, scale).astype(BF16))

    mid_steps = pl.num_programs(1) // MID
    table_local_a = jnp.where(step == 0, N_PAT, 0)
    table_mid_a = jnp.where(step % mid_steps == 0, N_PAT + 1, 1)

    def unit(q2, k2, v2, table, hp, own_block_only=False):
        q_both = jnp.concatenate([q2 * q_mask[0], q2 * q_mask[1]], axis=0)
        s_both = lax.dot_general(q_both, k2, (((1,), (1,)), ((), ())),
                                 preferred_element_type=F32)
        ps, ms, ls = [], [], []
        for j in range(2):
            hd = 2 * hp + j
            s = s_both[j * SPAN:(j + 1) * SPAN]
            if own_block_only:
                s = s + bias_scr[table, hd, :, SPAN:]
            else:
                s = s + bias_scr[table, hd]
            m = jnp.max(s, axis=-1, keepdims=True)
            p = jnp.exp(s - m)
            ls.append(jnp.sum(p, axis=-1, keepdims=True))
            ms.append(m)
            ps.append(p.astype(BF16))
        pv = jnp.dot(jnp.concatenate(ps, axis=0), v2, preferred_element_type=F32)
        o = jnp.where(low_half, pv[:SPAN] / ls[0], pv[SPAN:] / ls[1])
        return o, (ms, ls)

    def pair_body(hp, carry):
        l1a, l1b, l2a, l2b, l3a, l3b = carry
        cols = pl.ds(pl.multiple_of(hp * LANES, LANES), LANES)

        def put(tiles, stats):
            m_tile, l_tile = tiles
            for j in range(2):
                mine = lane_head == 2 * hp + j
                m_tile = jnp.where(mine, stats[0][j], m_tile)
                l_tile = jnp.where(mine, stats[1][j], l_tile)
            return m_tile, l_tile

        q1 = q1_ref[:, :, cols].astype(F32).reshape(RES, 2, HALF, LANES)
        q1a = q1[:, 0].reshape(SPAN, LANES).astype(BF16)
        q1b = q1[:, 1].reshape(SPAN, LANES).astype(BF16)
        k1c = k1c_ref[:, cols]
        v1c = v1c_ref[:, cols]
        oa, la = unit(q1a, jnp.concatenate([k1p_ref[:, cols], k1c[:SPAN]], axis=0),
                      jnp.concatenate([v1p_ref[:, cols], v1c[:SPAN]], axis=0),
                      table_local_a, hp)
        ob, lb = unit(q1b, k1c, v1c, 0, hp)
        o1 = jnp.concatenate([oa.reshape(RES, HALF, LANES), ob.reshape(RES, HALF, LANES)], axis=1)
        o1_ref[:, :, cols] = o1.astype(BF16)
        l1a, l1b = put(l1a, la), put(l1b, lb)

        q2 = q2_ref[:, :, cols]
        k2p, k2c = k2p_ref[:, :, cols], k2c_ref[:, :, cols]
        v2p, v2c = v2p_ref[:, :, cols], v2c_ref[:, :, cols]
        q2a = q2[:, :QUARTER].reshape(SPAN, LANES)
        q2b = q2[:, QUARTER:].reshape(SPAN, LANES)
        k2a = jnp.concatenate([k2p, k2c[:, :QUARTER]], axis=1).reshape(2 * SPAN, LANES)
        v2a = jnp.concatenate([v2p, v2c[:, :QUARTER]], axis=1).reshape(2 * SPAN, LANES)
        oa, la = unit(q2a, k2a, v2a, table_mid_a, hp)
        ob, lb = unit(q2b, k2c.reshape(2 * SPAN, LANES), v2c.reshape(2 * SPAN, LANES), 1, hp)
        o2_ref[:, :QUARTER, cols] = oa.reshape(SUB, QUARTER, LANES).astype(BF16)
        o2_ref[:, QUARTER:, cols] = ob.reshape(SUB, QUARTER, LANES).astype(BF16)
        l2a, l2b = put(l2a, la), put(l2b, lb)

        q3, k3, v3 = q3_ref[:, cols], k3_ref[:, cols], v3_ref[:, cols]
        oa, la = unit(q3[:SPAN], k3[:SPAN], v3[:SPAN], 2, hp, own_block_only=True)
        ob, lb = unit(q3[SPAN:], k3, v3, 2, hp)
        o3_ref[:SPAN, cols] = oa.astype(BF16)
        o3_ref[SPAN:, cols] = ob.astype(BF16)
        l3a, l3b = put(l3a, la), put(l3b, lb)
        return l1a, l1b, l2a, l2b, l3a, l3b

    init = (jnp.zeros((SPAN, LANES), F32), jnp.ones((SPAN, LANES), F32))
    stats = lax.fori_loop(0, B_HEADS // 2, pair_body, (init,) * 6, unroll=PAIR_UNROLL)
    l1a, l1b, l2a, l2b, l3a, l3b = [m + jnp.log(l) for m, l in stats]
    l1_ref[...] = jnp.concatenate(
        [l1a.reshape(RES, HALF, LANES), l1b.reshape(RES, HALF, LANES)], axis=1)
    l2_ref[:, :QUARTER] = l2a.reshape(SUB, QUARTER, LANES)
    l2_ref[:, QUARTER:] = l2b.reshape(SUB, QUARTER, LANES)
    l3_ref[:SPAN] = l3a
    l3_ref[SPAN:] = l3b


def _attention(qkv):
    batch, seq, n_qkv = qkv.shape
    width = B_HEADS * B_HEAD_DIM
    per = seq // RES
    n_steps = seq // (2 * SPAN)
    by_res = qkv.reshape(batch, RES, per, n_qkv)
    by_sub = qkv.reshape(batch, SUB, MID, per, n_qkv)

    def col(g, part):
        return g * 3 + part

    mid_steps = n_steps // MID

    def mid_spec(rows, part, prev):
        def index_map(b, s):
            blk = s % mid_steps
            if prev:
                blk = jnp.maximum(2 * blk - 1, 0)
            return (b, 0, s // mid_steps, blk, col(1, part))
        return pl.BlockSpec((None, SUB, None, rows, width), index_map)

    in_specs = [
        pl.BlockSpec((None, RES, 2 * HALF, width), lambda b, s: (b, 0, s, col(0, 0))),
        pl.BlockSpec((None, SPAN, width), lambda b, s: (b, jnp.maximum(2 * s - 1, 0), col(0, 1))),
        pl.BlockSpec((None, 2 * SPAN, width), lambda b, s: (b, s, col(0, 1))),
        pl.BlockSpec((None, SPAN, width), lambda b, s: (b, jnp.maximum(2 * s - 1, 0), col(0, 2))),
        pl.BlockSpec((None, 2 * SPAN, width), lambda b, s: (b, s, col(0, 2))),
        mid_spec(2 * QUARTER, 0, False),
        mid_spec(QUARTER, 1, True), mid_spec(2 * QUARTER, 1, False),
        mid_spec(QUARTER, 2, True), mid_spec(2 * QUARTER, 2, False),
        pl.BlockSpec((None, None, per, width), lambda b, s: (b, s, 0, col(2, 0))),
        pl.BlockSpec((None, None, per, width), lambda b, s: (b, s, 0, col(2, 1))),
        pl.BlockSpec((None, None, per, width), lambda b, s: (b, s, 0, col(2, 2))),
    ]
    operands = [by_res, qkv, qkv, qkv, qkv] + [by_sub] * 5 + [by_res] * 3

    out_specs, out_shapes = [], []

    def outs_for(w, dtype):
        return (
            [pl.BlockSpec((None, RES, 2 * HALF, w), lambda b, s: (b, 0, s, 0)),
             pl.BlockSpec((None, SUB, None, 2 * QUARTER, w),
                          lambda b, s: (b, 0, s // mid_steps, s % mid_steps, 0)),
             pl.BlockSpec((None, None, per, w), lambda b, s: (b, s, 0, 0))],
            [jax.ShapeDtypeStruct((batch, RES, per, w), dtype),
             jax.ShapeDtypeStruct((batch, SUB, MID, per, w), dtype),
             jax.ShapeDtypeStruct((batch, RES, per, w), dtype)])
    o_specs, o_shapes = outs_for(width, BF16)
    l_specs, l_shapes = outs_for(LANES, F32)
    for g in range(N_PAT):
        out_specs += [o_specs[g], l_specs[g]]
        out_shapes += [o_shapes[g], l_shapes[g]]

    outs = pl.pallas_call(
        _attn_kernel,
        grid=(batch, n_steps),
        in_specs=in_specs,
        out_specs=out_specs,
        out_shape=out_shapes,
        scratch_shapes=[pltpu.VMEM((N_PAT + 2, B_HEADS, SPAN, 2 * SPAN), F32)],
        compiler_params=pltpu.CompilerParams(
            dimension_semantics=("arbitrary", "arbitrary"),
            vmem_limit_bytes=VMEM_LIMIT_BYTES),
        name="dilated_attn",
    )(*operands)
    o = [outs[2 * g].reshape(batch, seq, width) for g in range(N_PAT)]
    lse = [outs[2 * g + 1].reshape(batch, seq, LANES) for g in range(N_PAT)]
    return o, lse


def _attn_out_kernel(o0_ref, o1_ref, o2_ref, l0_ref, l1_ref, l2_ref, x_ref, mod_ref, w_out_ref,
                     ln_g_ref, ln_b_ref, out_ref):
    lses = [l0_ref[...], l1_ref[...], l2_ref[...]]
    mx = jnp.maximum(jnp.maximum(lses[0], lses[1]), lses[2])
    es = [jnp.exp(l - mx) for l in lses]
    den = es[0] + es[1] + es[2]
    src_lane = lax.broadcasted_iota(jnp.int32, (LANES, D_MODEL), 0)
    head_of_col = lax.broadcasted_iota(jnp.int32, (LANES, D_MODEL), 1) // B_HEAD_DIM
    expand = jnp.logical_and(src_lane < 2 * B_HEADS, src_lane % B_HEADS == head_of_col)
    expand = expand.astype(BF16)
    lane = lax.broadcasted_iota(jnp.int32, lses[0].shape, 1)
    o = None
    for e, o_ref in zip(es, (o0_ref, o1_ref, o2_ref)):
        wts = e / den
        head = wts.astype(BF16)
        rest = (wts - head.astype(F32)).astype(BF16)
        split = jnp.where(lane < B_HEADS, head, jnp.where(lane < 2 * B_HEADS, rest, 0.0))
        w = jnp.dot(split.astype(BF16), expand, preferred_element_type=F32)
        term = w * o_ref[...].astype(F32)
        o = term if o is None else o + term
    y = jnp.dot(o.astype(BF16), w_out_ref[...], preferred_element_type=F32)
    mod = mod_ref[...]
    out_ref[...] = _post_norm(x_ref[...], y, mod, ln_g_ref[...], ln_b_ref[...])


def _attn_out(o, lse, x, mod, w_out, ln_g, ln_b):
    batch, seq, _ = x.shape
    tm = 512
    row = lambda a: a.reshape(1, -1)
    tile = lambda w: pl.BlockSpec((None, tm, w), lambda b, i: (b, i, 0))
    return pl.pallas_call(
        _attn_out_kernel,
        grid=(batch, seq // tm),
        in_specs=[tile(D_MODEL)] * 3 + [tile(LANES)] * 3 + [
            tile(D_MODEL),
            pl.BlockSpec((None, 3, D_MODEL), lambda b, i: (b, 0, 0)),
            _resident((D_MODEL, D_MODEL)),
            _resident((1, D_MODEL)),
            _resident((1, D_MODEL)),
        ],
        out_specs=tile(D_MODEL),
        out_shape=jax.ShapeDtypeStruct(x.shape, F32),
        compiler_params=pltpu.CompilerParams(
            dimension_semantics=("arbitrary", "arbitrary"),
            vmem_limit_bytes=VMEM_LIMIT_BYTES),
        name="attn_out",
    )(*o, *lse, x, mod, w_out.astype(BF16), row(ln_g), row(ln_b))


def kernel(x, c, ada_w, ada_b, ln_g, ln_b, a_w_in, a_b_in, a_vn_g, a_vn_b, a_w_s, a_b_s, a_w_out,
           b_w_qkv, b_w_out, mlp_w_up, mlp_w_down):
    mods = _ada_mod(c, ada_w, ada_b)
    x = _mixer_a(x, mods[0], a_w_in[0], a_b_in[0], a_vn_g[0], a_vn_b[0], a_w_s[0], a_b_s[0],
                 a_w_out[0], ln_g[0, 0], ln_b[0, 0])
    x, x_res = _mlp(x, mods[1], mlp_w_up[0], mlp_w_down[0], ln_g[0, 1], ln_b[0, 1],
                    residue_in=False)
    qkv = _qkv(x, x_res, mods[2], b_w_qkv[0])
    o, lse = _attention(qkv)
    x_res = _attn_out(o, lse, x_res, mods[2], b_w_out[0], ln_g[1, 0], ln_b[1, 0])
    return _mlp(x_res, mods[3], mlp_w_up[1], mlp_w_down[1], ln_g[1, 1], ln_b[1, 1],
                residue_in=True)
```

```python
import functools

import jax
import jax.numpy as jnp
from jax import lax
from jax.experimental import pallas as pl
from jax.experimental.pallas import tpu as pltpu

D_MODEL = 1024
DEPTH = 2
CHUNK = 128
A_GROUPS = 16
A_GROUP_DIM = D_MODEL // A_GROUPS
B_HEADS = 16
B_HEAD_DIM = D_MODEL // B_HEADS
B_PATTERNS = ((128, 1), (512, 4), (2048, 16))
N_PAT = len(B_PATTERNS)
SPAN = 128
D_FF = 4 * D_MODEL
ALPHA = (2 * DEPTH) ** 0.25
LN_EPS = 1e-5
NEG = -1e30

LANES = 128
SUBLANES = 8
VMEM_LIMIT_BYTES = 56 * 1024 * 1024

RES = 16
MID = 4
SUB = RES // MID

F32 = jnp.float32
BF16 = jnp.bfloat16


def _layer_norm(x, g, b):
    mu = jnp.mean(x, axis=-1, keepdims=True)
    xc = x - mu
    var = jnp.mean(xc * xc, axis=-1, keepdims=True)
    return xc * lax.rsqrt(var + LN_EPS) * g + b


def _resident(shape):
    zeros = (0,) * len(shape)
    return pl.BlockSpec(shape, lambda *_: zeros, pipeline_mode=pl.Buffered(1))


def _ada_kernel(c_ref, w_ref, b_ref, o_ref):
    s = jax.nn.silu(c_ref[...]).astype(BF16)
    o_ref[0] = jnp.dot(s, w_ref[0].astype(BF16), preferred_element_type=F32) + b_ref[0]


def _ada_mod(c, ada_w, ada_b):
    batch = c.shape[0]
    rows = SUBLANES
    c_pad = jnp.zeros((rows, D_MODEL), F32).at[:batch].set(c)
    n_sub = DEPTH * 2
    w = ada_w.reshape(n_sub, D_MODEL, 3 * D_MODEL)
    b = ada_b.reshape(n_sub, 1, 3 * D_MODEL)
    tn = 1024
    out = pl.pallas_call(
        _ada_kernel,
        grid=(n_sub, 3 * D_MODEL // tn),
        in_specs=[
            pl.BlockSpec((rows, D_MODEL), lambda i, j: (0, 0)),
            pl.BlockSpec((1, D_MODEL, tn), lambda i, j: (i, 0, j)),
            pl.BlockSpec((1, 1, tn), lambda i, j: (i, 0, j)),
        ],
        out_specs=pl.BlockSpec((1, rows, tn), lambda i, j: (i, 0, j)),
        out_shape=jax.ShapeDtypeStruct((n_sub, rows, 3 * D_MODEL), F32),
        name="ada_mod",
    )(c_pad, w, b)
    return out[:, :batch].reshape(n_sub, batch, 3, D_MODEL)


def _modulate(x, mod):
    shift = mod[0:1]
    scale = mod[1:2]
    return (x * (1.0 + scale) + shift).astype(BF16)


def _post_norm(x, y, mod, g, b):
    gate = 1.0 + mod[2:3]
    return _layer_norm(ALPHA * x + gate * y, g, b)


def _mixer_a_kernel(x_ref, mod_ref, w_in_ref, b_in_ref, vn_g_ref, vn_b_ref, ws_ref, bs_ref,
                    w_out_ref, ln_g_ref, ln_b_ref, o_ref, ws_scr, u_scr, v_scr, g_scr):
    tm = x_ref.shape[0]

    @pl.when(jnp.logical_and(pl.program_id(0) == 0, pl.program_id(1) == 0))
    def _():
        t = lax.broadcasted_iota(jnp.int32, (CHUNK, CHUNK), 0)
        s = lax.broadcasted_iota(jnp.int32, (CHUNK, CHUNK), 1)
        causal = s <= t
        for g in range(A_GROUPS):
            ws_scr[g] = jnp.where(causal, ws_ref[g], 0.0).astype(BF16)

    x = x_ref[...]
    mod = mod_ref[...]
    h = _modulate(x, mod)
    v = jnp.dot(h, w_in_ref[:, D_MODEL:], preferred_element_type=F32) + b_in_ref[:, D_MODEL:]
    u = jnp.dot(h, w_in_ref[:, :D_MODEL], preferred_element_type=F32) + b_in_ref[:, :D_MODEL]
    v = _layer_norm(jax.nn.gelu(v), vn_g_ref[...], vn_b_ref[...])
    v_scr[...] = v.astype(BF16)
    u_scr[...] = jax.nn.gelu(u)

    low_half = lax.broadcasted_iota(jnp.int32, (CHUNK, LANES), 1) < A_GROUP_DIM

    def chunk_body(c, carry):
        r0 = pl.multiple_of(c * CHUNK, CHUNK)
        for gp in range(A_GROUPS // 2):
            cols = slice(gp * LANES, (gp + 1) * LANES)
            vb = v_scr[pl.ds(r0, CHUNK), cols]
            za = jnp.dot(ws_scr[2 * gp], vb, preferred_element_type=F32)
            zb = jnp.dot(ws_scr[2 * gp + 1], vb, preferred_element_type=F32)
            z = jnp.where(low_half, za, zb) + bs_ref[:, cols]
            u = u_scr[pl.ds(r0, CHUNK), cols]
            g_scr[pl.ds(r0, CHUNK), cols] = (u * z).astype(BF16)
        return carry

    lax.fori_loop(0, tm // CHUNK, chunk_body, 0, unroll=True)

    y = jnp.dot(g_scr[...], w_out_ref[...], preferred_element_type=F32)
    o_ref[...] = _post_norm(x, y, mod, ln_g_ref[...], ln_b_ref[...])


def _mixer_a(x, mod, w_in, b_in, vn_g, vn_b, w_s, b_s, w_out, ln_g, ln_b):
    batch, seq, _ = x.shape
    tm = 512
    row = lambda a: a.reshape(1, -1)
    bs_plane = jnp.repeat(b_s.T, A_GROUP_DIM, axis=1)
    return pl.pallas_call(
        _mixer_a_kernel,
        grid=(batch, seq // tm),
        in_specs=[
            pl.BlockSpec((None, tm, D_MODEL), lambda b, i: (b, i, 0)),
            pl.BlockSpec((None, 3, D_MODEL), lambda b, i: (b, 0, 0)),
            _resident((D_MODEL, 2 * D_MODEL)),
            _resident((1, 2 * D_MODEL)),
            _resident((1, D_MODEL)),
            _resident((1, D_MODEL)),
            _resident((A_GROUPS, CHUNK, CHUNK)),
            _resident((CHUNK, D_MODEL)),
            _resident((D_MODEL, D_MODEL)),
            _resident((1, D_MODEL)),
            _resident((1, D_MODEL)),
        ],
        out_specs=pl.BlockSpec((None, tm, D_MODEL), lambda b, i: (b, i, 0)),
        out_shape=jax.ShapeDtypeStruct(x.shape, F32),
        scratch_shapes=[
            pltpu.VMEM((A_GROUPS, CHUNK, CHUNK), BF16),
            pltpu.VMEM((tm, D_MODEL), F32),
            pltpu.VMEM((tm, D_MODEL), BF16),
            pltpu.VMEM((tm, D_MODEL), BF16),
        ],
        compiler_params=pltpu.CompilerParams(
            dimension_semantics=("arbitrary", "arbitrary"),
            vmem_limit_bytes=VMEM_LIMIT_BYTES),
        name="mixer_a",
    )(x, mod, w_in.astype(BF16), row(b_in), row(vn_g), row(vn_b), w_s, bs_plane,
      w_out.astype(BF16), row(ln_g), row(ln_b))


def _mlp_kernel(residue_in, x_ref, mod_ref, w_up_ref, w_down_ref, ln_g_ref, ln_b_ref, *o_refs):
    tm = o_refs[0].shape[0]
    x = x_ref[...].reshape(tm, D_MODEL)
    mod = mod_ref[...]
    h = _modulate(x, mod)
    a = jnp.dot(h, w_up_ref[...], preferred_element_type=F32)
    r = jnp.square(jnp.maximum(a, 0.0)).astype(BF16)
    y = jnp.dot(r, w_down_ref[...], preferred_element_type=F32)
    out = _post_norm(x, y, mod, ln_g_ref[...], ln_b_ref[...])
    if residue_in:
        (nat_ref,) = o_refs
        nat_ref[...] = jnp.swapaxes(out.reshape(RES, tm // RES, D_MODEL), 0, 1).reshape(tm, D_MODEL)
    else:
        nat_ref, res_ref = o_refs
        nat_ref[...] = out
        res_ref[...] = jnp.swapaxes(out.reshape(tm // RES, RES, D_MODEL), 0, 1)


def _mlp(x, mod, w_up, w_down, ln_g, ln_b, residue_in):
    batch, seq, _ = x.shape
    tm = 512
    row = lambda a: a.reshape(1, -1)
    nat_spec = pl.BlockSpec((None, tm, D_MODEL), lambda b, i: (b, i, 0))
    res_spec = pl.BlockSpec((None, RES, tm // RES, D_MODEL), lambda b, i: (b, 0, i, 0))
    nat_shape = jax.ShapeDtypeStruct((batch, seq, D_MODEL), F32)
    res_shape = jax.ShapeDtypeStruct((batch, RES, seq // RES, D_MODEL), F32)
    if residue_in:
        x = x.reshape(res_shape.shape)
    outs = pl.pallas_call(
        functools.partial(_mlp_kernel, residue_in),
        grid=(batch, seq // tm),
        in_specs=[
            res_spec if residue_in else nat_spec,
            pl.BlockSpec((None, 3, D_MODEL), lambda b, i: (b, 0, 0)),
            _resident((D_MODEL, D_FF)),
            _resident((D_FF, D_MODEL)),
            _resident((1, D_MODEL)),
            _resident((1, D_MODEL)),
        ],
        out_specs=[nat_spec] if residue_in else [nat_spec, res_spec],
        out_shape=[nat_shape] if residue_in else [nat_shape, res_shape],
        compiler_params=pltpu.CompilerParams(
            dimension_semantics=("arbitrary", "arbitrary"),
            vmem_limit_bytes=VMEM_LIMIT_BYTES),
        name="mlp",
    )(x, mod, w_up.astype(BF16), w_down.astype(BF16), row(ln_g), row(ln_b))
    if residue_in:
        return outs[0]
    return outs[0], outs[1].reshape(batch, seq, D_MODEL)


LOCAL_KV_BLOCKS = (1, 2)


def _qkv_kernel(x_nat_ref, x_res_ref, mod_ref, w_ref, o_ref):
    mod = mod_ref[...]
    h_nat = _modulate(x_nat_ref[...], mod)
    h_res = _modulate(x_res_ref[...], mod)
    width = B_HEADS * B_HEAD_DIM
    for c in range(o_ref.shape[1] // width):
        cols = slice(c * width, (c + 1) * width)
        h = h_nat if c in LOCAL_KV_BLOCKS else h_res
        o_ref[:, cols] = jnp.dot(h, w_ref[:, cols], preferred_element_type=F32).astype(BF16)


def _qkv(x_nat, x_res, mod, w_qkv):
    batch, seq, _ = x_nat.shape
    n_out = w_qkv.shape[1]
    tm = 512
    return pl.pallas_call(
        _qkv_kernel,
        grid=(batch, seq // tm),
        in_specs=[
            pl.BlockSpec((None, tm, D_MODEL), lambda b, i: (b, i, 0)),
            pl.BlockSpec((None, tm, D_MODEL), lambda b, i: (b, i, 0)),
            pl.BlockSpec((None, 3, D_MODEL), lambda b, i: (b, 0, 0)),
            _resident((D_MODEL, n_out)),
        ],
        out_specs=pl.BlockSpec((None, tm, n_out), lambda b, i: (b, i, 0)),
        out_shape=jax.ShapeDtypeStruct((batch, seq, n_out), BF16),
        compiler_params=pltpu.CompilerParams(
            dimension_semantics=("arbitrary", "arbitrary"),
            vmem_limit_bytes=VMEM_LIMIT_BYTES),
        name="qkv_proj",
    )(x_nat, x_res, mod, w_qkv.astype(BF16))


HALF = SPAN // RES
QUARTER = SPAN // SUB
UNITS = 2 * N_PAT


def _build_bias_tables(bias_scr):
    row = lax.broadcasted_iota(jnp.int32, (SPAN, 2 * SPAN), 0)
    col = lax.broadcasted_iota(jnp.int32, (SPAN, 2 * SPAN), 1)
    diff_local = SPAN + RES * (row % HALF) + row // HALF - col
    diff_mid = (SUB * (row % QUARTER) + row // QUARTER
                - SUB * (col % (2 * QUARTER) - QUARTER) - col // (2 * QUARTER))
    diff_wide = SPAN + row - col
    has_prev = (None, None, None, col >= SPAN, col % (2 * QUARTER) >= QUARTER)
    for t, diff in enumerate((diff_local, diff_mid, diff_wide, diff_local, diff_mid)):
        dilation = B_PATTERNS[t % N_PAT][1]
        valid = jnp.logical_and(diff >= 0, diff <= SPAN)
        if has_prev[t] is not None:
            valid = jnp.logical_and(valid, has_prev[t])
        dist = (dilation * diff).astype(F32)
        for hd in range(B_HEADS):
            slope = 2.0 ** (-8.0 * (hd + 1) / B_HEADS)
            bias_scr[t, hd] = jnp.where(valid, -slope * dist, NEG)


def _attn_kernel(q1_ref, k1p_ref, k1c_ref, v1p_ref, v1c_ref,
                 q2_ref, k2p_ref, k2c_ref, v2p_ref, v2c_ref,
                 q3_ref, k3_ref, v3_ref,
                 o1_ref, l1_ref, o2_ref, l2_ref, o3_ref, l3_ref, bias_scr, m_scr, l_scr):
    step = pl.program_id(1)

    @pl.when(jnp.logical_and(pl.program_id(0) == 0, step == 0))
    def _():
        _build_bias_tables(bias_scr)
        m_scr[...] = jnp.zeros(m_scr.shape, F32)
        l_scr[...] = jnp.ones(l_scr.shape, F32)

    lane = lax.broadcasted_iota(jnp.int32, (SPAN, LANES), 1)
    low_half = lane < B_HEAD_DIM
    scale = B_HEAD_DIM ** -0.5
    q_mask = (jnp.where(low_half, scale, 0.0).astype(BF16),
              jnp.where(low_half, 0.0, scale).astype(BF16))

    mid_steps = pl.num_programs(1) // MID
    table_local_a = jnp.where(step == 0, N_PAT, 0)
    table_mid_a = jnp.where(step % mid_steps == 0, N_PAT + 1, 1)

    def unit(q2, k2, v2, table, hp, own_block_only=False):
        q_both = jnp.concatenate([q2 * q_mask[0], q2 * q_mask[1]], axis=0)
        s_both = lax.dot_general(q_both, k2, (((1,), (1,)), ((), ())),
                                 preferred_element_type=F32)
        ps, ms, ls = [], [], []
        for j in range(2):
            hd = 2 * hp + j
            s = s_both[j * SPAN:(j + 1) * SPAN]
            if own_block_only:
                s = s + bias_scr[table, hd, :, SPAN:]
            else:
                s = s + bias_scr[table, hd]
            m = jnp.max(s, axis=-1, keepdims=True)
            p = jnp.exp(s - m)
            ls.append(jnp.sum(p, axis=-1, keepdims=True))
            ms.append(m)
            ps.append(p.astype(BF16))
        pv = jnp.dot(jnp.concatenate(ps, axis=0), v2, preferred_element_type=F32)
        o = jnp.where(low_half, pv[:SPAN] / ls[0], pv[SPAN:] / ls[1])
        return o, (ms, ls)

    def put(u, hp, stats):
        for j in range(2):
            hd = 2 * hp + j
            m_scr[u, :, hd:hd + 1] = stats[0][j]
            l_scr[u, :, hd:hd + 1] = stats[1][j]

    for hp in range(B_HEADS // 2):
        cols = slice(hp * LANES, (hp + 1) * LANES)

        q1 = q1_ref[:, :, cols].astype(F32).reshape(RES, 2, HALF, LANES)
        q1a = q1[:, 0].reshape(SPAN, LANES).astype(BF16)
        q1b = q1[:, 1].reshape(SPAN, LANES).astype(BF16)
        k1c = k1c_ref[:, cols]
        v1c = v1c_ref[:, cols]
        oa, sa = unit(q1a, jnp.concatenate([k1p_ref[:, cols], k1c[:SPAN]], axis=0),
                      jnp.concatenate([v1p_ref[:, cols], v1c[:SPAN]], axis=0),
                      table_local_a, hp)
        ob, sb = unit(q1b, k1c, v1c, 0, hp)
        o1 = jnp.concatenate([oa.reshape(RES, HALF, LANES), ob.reshape(RES, HALF, LANES)], axis=1)
        o1_ref[:, :, cols] = o1.astype(BF16)
        put(0, hp, sa)
        put(1, hp, sb)

        q2 = q2_ref[:, :, cols]
        k2p, k2c = k2p_ref[:, :, cols], k2c_ref[:, :, cols]
        v2p, v2c = v2p_ref[:, :, cols], v2c_ref[:, :, cols]
        q2a = q2[:, :QUARTER].reshape(SPAN, LANES)
        q2b = q2[:, QUARTER:].reshape(SPAN, LANES)
        k2a = jnp.concatenate([k2p, k2c[:, :QUARTER]], axis=1).reshape(2 * SPAN, LANES)
        v2a = jnp.concatenate([v2p, v2c[:, :QUARTER]], axis=1).reshape(2 * SPAN, LANES)
        oa, sa = unit(q2a, k2a, v2a, table_mid_a, hp)
        ob, sb = unit(q2b, k2c.reshape(2 * SPAN, LANES), v2c.reshape(2 * SPAN, LANES), 1, hp)
        o2_ref[:, :QUARTER, cols] = oa.reshape(SUB, QUARTER, LANES).astype(BF16)
        o2_ref[:, QUARTER:, cols] = ob.reshape(SUB, QUARTER, LANES).astype(BF16)
        put(2, hp, sa)
        put(3, hp, sb)

        q3, k3, v3 = q3_ref[:, cols], k3_ref[:, cols], v3_ref[:, cols]
        oa, sa = unit(q3[:SPAN], k3[:SPAN], v3[:SPAN], 2, hp, own_block_only=True)
        ob, sb = unit(q3[SPAN:], k3, v3, 2, hp)
        o3_ref[:SPAN, cols] = oa.astype(BF16)
        o3_ref[SPAN:, cols] = ob.astype(BF16)
        put(4, hp, sa)
        put(5, hp, sb)

    l1a, l1b, l2a, l2b, l3a, l3b = [m_scr[u] + jnp.log(l_scr[u]) for u in range(UNITS)]
    l1_ref[...] = jnp.concatenate(
        [l1a.reshape(RES, HALF, LANES), l1b.reshape(RES, HALF, LANES)], axis=1)
    l2_ref[:, :QUARTER] = l2a.reshape(SUB, QUARTER, LANES)
    l2_ref[:, QUARTER:] = l2b.reshape(SUB, QUARTER, LANES)
    l3_ref[:SPAN] = l3a
    l3_ref[SPAN:] = l3b


def _attention(qkv):
    batch, seq, n_qkv = qkv.shape
    width = B_HEADS * B_HEAD_DIM
    per = seq // RES
    n_steps = seq // (2 * SPAN)
    by_res = qkv.reshape(batch, RES, per, n_qkv)
    by_sub = qkv.reshape(batch, SUB, MID, per, n_qkv)

    def col(g, part):
        return g * 3 + part

    mid_steps = n_steps // MID

    def mid_spec(rows, part, prev):
        def index_map(b, s):
            blk = s % mid_steps
            if prev:
                blk = jnp.maximum(2 * blk - 1, 0)
            return (b, 0, s // mid_steps, blk, col(1, part))
        return pl.BlockSpec((None, SUB, None, rows, width), index_map)

    in_specs = [
        pl.BlockSpec((None, RES, 2 * HALF, width), lambda b, s: (b, 0, s, col(0, 0))),
        pl.BlockSpec((None, SPAN, width), lambda b, s: (b, jnp.maximum(2 * s - 1, 0), col(0, 1))),
        pl.BlockSpec((None, 2 * SPAN, width), lambda b, s: (b, s, col(0, 1))),
        pl.BlockSpec((None, SPAN, width), lambda b, s: (b, jnp.maximum(2 * s - 1, 0), col(0, 2))),
        pl.BlockSpec((None, 2 * SPAN, width), lambda b, s: (b, s, col(0, 2))),
        mid_spec(2 * QUARTER, 0, False),
        mid_spec(QUARTER, 1, True), mid_spec(2 * QUARTER, 1, False),
        mid_spec(QUARTER, 2, True), mid_spec(2 * QUARTER, 2, False),
        pl.BlockSpec((None, None, per, width), lambda b, s: (b, s, 0, col(2, 0))),
        pl.BlockSpec((None, None, per, width), lambda b, s: (b, s, 0, col(2, 1))),
        pl.BlockSpec((None, None, per, width), lambda b, s: (b, s, 0, col(2, 2))),
    ]
    operands = [by_res, qkv, qkv, qkv, qkv] + [by_sub] * 5 + [by_res] * 3

    out_specs, out_shapes = [], []

    def outs_for(w, dtype):
        return (
            [pl.BlockSpec((None, RES, 2 * HALF, w), lambda b, s: (b, 0, s, 0)),
             pl.BlockSpec((None, SUB, None, 2 * QUARTER, w),
                          lambda b, s: (b, 0, s // mid_steps, s % mid_steps, 0)),
             pl.BlockSpec((None, None, per, w), lambda b, s: (b, s, 0, 0))],
            [jax.ShapeDtypeStruct((batch, RES, per, w), dtype),
             jax.ShapeDtypeStruct((batch, SUB, MID, per, w), dtype),
             jax.ShapeDtypeStruct((batch, RES, per, w), dtype)])
    o_specs, o_shapes = outs_for(width, BF16)
    l_specs, l_shapes = outs_for(LANES, F32)
    for g in range(N_PAT):
        out_specs += [o_specs[g], l_specs[g]]
        out_shapes += [o_shapes[g], l_shapes[g]]

    outs = pl.pallas_call(
        _attn_kernel,
        grid=(batch, n_steps),
        in_specs=in_specs,
        out_specs=out_specs,
        out_shape=out_shapes,
        scratch_shapes=[pltpu.VMEM((N_PAT + 2, B_HEADS, SPAN, 2 * SPAN), F32),
                        pltpu.VMEM((UNITS, SPAN, LANES), F32),
                        pltpu.VMEM((UNITS, SPAN, LANES), F32)],
        compiler_params=pltpu.CompilerParams(
            dimension_semantics=("arbitrary", "arbitrary"),
            vmem_limit_bytes=VMEM_LIMIT_BYTES),
        name="dilated_attn",
    )(*operands)
    o = [outs[2 * g].reshape(batch, seq, width) for g in range(N_PAT)]
    lse = [outs[2 * g + 1].reshape(batch, seq, LANES) for g in range(N_PAT)]
    return o, lse


def _attn_out_kernel(o0_ref, o1_ref, o2_ref, l0_ref, l1_ref, l2_ref, x_ref, mod_ref, w_out_ref,
                     ln_g_ref, ln_b_ref, out_ref):
    lses = [l0_ref[...], l1_ref[...], l2_ref[...]]
    mx = jnp.maximum(jnp.maximum(lses[0], lses[1]), lses[2])
    es = [jnp.exp(l - mx) for l in lses]
    den = es[0] + es[1] + es[2]
    src_lane = lax.broadcasted_iota(jnp.int32, (LANES, D_MODEL), 0)
    head_of_col = lax.broadcasted_iota(jnp.int32, (LANES, D_MODEL), 1) // B_HEAD_DIM
    expand = jnp.logical_and(src_lane < 2 * B_HEADS, src_lane % B_HEADS == head_of_col)
    expand = expand.astype(BF16)
    lane = lax.broadcasted_iota(jnp.int32, lses[0].shape, 1)
    o = None
    for e, o_ref in zip(es, (o0_ref, o1_ref, o2_ref)):
        wts = e / den
        head = wts.astype(BF16).astype(F32)
        rest = pltpu.roll(wts - head, B_HEADS, axis=1)
        split = jnp.where(lane < B_HEADS, head, jnp.where(lane < 2 * B_HEADS, rest, 0.0))
        w = jnp.dot(split.astype(BF16), expand, preferred_element_type=F32)
        term = w * o_ref[...].astype(F32)
        o = term if o is None else o + term
    y = jnp.dot(o.astype(BF16), w_out_ref[...], preferred_element_type=F32)
    mod = mod_ref[...]
    out_ref[...] = _post_norm(x_ref[...], y, mod, ln_g_ref[...], ln_b_ref[...])


def _attn_out(o, lse, x, mod, w_out, ln_g, ln_b):
    batch, seq, _ = x.shape
    tm = 512
    row = lambda a: a.reshape(1, -1)
    tile = lambda w: pl.BlockSpec((None, tm, w), lambda b, i: (b, i, 0))
    return pl.pallas_call(
        _attn_out_kernel,
        grid=(batch, seq // tm),
        in_specs=[tile(D_MODEL)] * 3 + [tile(LANES)] * 3 + [
            tile(D_MODEL),
            pl.BlockSpec((None, 3, D_MODEL), lambda b, i: (b, 0, 0)),
            _resident((D_MODEL, D_MODEL)),
            _resident((1, D_MODEL)),
            _resident((1, D_MODEL)),
        ],
        out_specs=tile(D_MODEL),
        out_shape=jax.ShapeDtypeStruct(x.shape, F32),
        compiler_params=pltpu.CompilerParams(
            dimension_semantics=("arbitrary", "arbitrary"),
            vmem_limit_bytes=VMEM_LIMIT_BYTES),
        name="attn_out",
    )(*o, *lse, x, mod, w_out.astype(BF16), row(ln_g), row(ln_b))


def kernel(x, c, ada_w, ada_b, ln_g, ln_b, a_w_in, a_b_in, a_vn_g, a_vn_b, a_w_s, a_b_s, a_w_out,
           b_w_qkv, b_w_out, mlp_w_up, mlp_w_down):
    mods = _ada_mod(c, ada_w, ada_b)
    x = _mixer_a(x, mods[0], a_w_in[0], a_b_in[0], a_vn_g[0], a_vn_b[0], a_w_s[0], a_b_s[0],
                 a_w_out[0], ln_g[0, 0], ln_b[0, 0])
    x, x_res = _mlp(x, mods[1], mlp_w_up[0], mlp_w_down[0], ln_g[0, 1], ln_b[0, 1],
                    residue_in=False)
    qkv = _qkv(x, x_res, mods[2], b_w_qkv[0])
    o, lse = _attention(qkv)
    x_res = _attn_out(o, lse, x_res, mods[2], b_w_out[0], ln_g[1, 0], ln_b[1, 0])
    return _mlp(x_res, mods[3], mlp_w_up[1], mlp_w_down[1], ln_g[1, 1], ln_b[1, 1],
                residue_in=True)
```

```python
import functools

import jax
import jax.numpy as jnp
from jax import lax
from jax.experimental import pallas as pl
from jax.experimental.pallas import tpu as pltpu

D_MODEL = 1024
DEPTH = 2
CHUNK = 128
A_GROUPS = 16
A_GROUP_DIM = D_MODEL // A_GROUPS
B_HEADS = 16
B_HEAD_DIM = D_MODEL // B_HEADS
B_PATTERNS = ((128, 1), (512, 4), (2048, 16))
N_PAT = len(B_PATTERNS)
SPAN = 128
D_FF = 4 * D_MODEL
ALPHA = (2 * DEPTH) ** 0.25
LN_EPS = 1e-5
NEG = -1e30

LANES = 128
SUBLANES = 8
VMEM_LIMIT_BYTES = 56 * 1024 * 1024

RES = 16
MID = 4
SUB = RES // MID

F32 = jnp.float32
BF16 = jnp.bfloat16


def _layer_norm(x, g, b):
    mu = jnp.mean(x, axis=-1, keepdims=True)
    xc = x - mu
    var = jnp.mean(xc * xc, axis=-1, keepdims=True)
    return xc * lax.rsqrt(var + LN_EPS) * g + b


def _resident(shape):
    zeros = (0,) * len(shape)
    return pl.BlockSpec(shape, lambda *_: zeros, pipeline_mode=pl.Buffered(1))


def _ada_kernel(c_ref, w_ref, b_ref, o_ref):
    s = jax.nn.silu(c_ref[...]).astype(BF16)
    o_ref[0] = jnp.dot(s, w_ref[0].astype(BF16), preferred_element_type=F32) + b_ref[0]


def _ada_mod(c, ada_w, ada_b):
    batch = c.shape[0]
    rows = SUBLANES
    c_pad = jnp.zeros((rows, D_MODEL), F32).at[:batch].set(c)
    n_sub = DEPTH * 2
    w = ada_w.reshape(n_sub, D_MODEL, 3 * D_MODEL)
    b = ada_b.reshape(n_sub, 1, 3 * D_MODEL)
    tn = 1024
    out = pl.pallas_call(
        _ada_kernel,
        grid=(n_sub, 3 * D_MODEL // tn),
        in_specs=[
            pl.BlockSpec((rows, D_MODEL), lambda i, j: (0, 0)),
            pl.BlockSpec((1, D_MODEL, tn), lambda i, j: (i, 0, j)),
            pl.BlockSpec((1, 1, tn), lambda i, j: (i, 0, j)),
        ],
        out_specs=pl.BlockSpec((1, rows, tn), lambda i, j: (i, 0, j)),
        out_shape=jax.ShapeDtypeStruct((n_sub, rows, 3 * D_MODEL), F32),
        name="ada_mod",
    )(c_pad, w, b)
    return out[:, :batch].reshape(n_sub, batch, 3, D_MODEL)


def _modulate(x, mod):
    shift = mod[0:1]
    scale = mod[1:2]
    return (x * (1.0 + scale) + shift).astype(BF16)


def _post_norm(x, y, mod, g, b):
    gate = 1.0 + mod[2:3]
    return _layer_norm(ALPHA * x + gate * y, g, b)


def _mixer_a_kernel(x_ref, mod_ref, w_in_ref, b_in_ref, vn_g_ref, vn_b_ref, ws_ref, bs_ref,
                    w_out_ref, ln_g_ref, ln_b_ref, o_ref, ws_scr, u_scr, v_scr, g_scr):
    tm = x_ref.shape[0]

    @pl.when(jnp.logical_and(pl.program_id(0) == 0, pl.program_id(1) == 0))
    def _():
        t = lax.broadcasted_iota(jnp.int32, (CHUNK, CHUNK), 0)
        s = lax.broadcasted_iota(jnp.int32, (CHUNK, CHUNK), 1)
        causal = s <= t
        for g in range(A_GROUPS):
            ws_scr[g] = jnp.where(causal, ws_ref[g], 0.0).astype(BF16)

    x = x_ref[...]
    mod = mod_ref[...]
    h = _modulate(x, mod)
    v = jnp.dot(h, w_in_ref[:, D_MODEL:], preferred_element_type=F32) + b_in_ref[:, D_MODEL:]
    u = jnp.dot(h, w_in_ref[:, :D_MODEL], preferred_element_type=F32) + b_in_ref[:, :D_MODEL]
    v = _layer_norm(jax.nn.gelu(v), vn_g_ref[...], vn_b_ref[...])
    v_scr[...] = v.astype(BF16)
    u_scr[...] = jax.nn.gelu(u)

    low_half = lax.broadcasted_iota(jnp.int32, (CHUNK, LANES), 1) < A_GROUP_DIM

    def chunk_body(c, carry):
        r0 = pl.multiple_of(c * CHUNK, CHUNK)
        for gp in range(A_GROUPS // 2):
            cols = slice(gp * LANES, (gp + 1) * LANES)
            vb = v_scr[pl.ds(r0, CHUNK), cols]
            za = jnp.dot(ws_scr[2 * gp], vb, preferred_element_type=F32)
            zb = jnp.dot(ws_scr[2 * gp + 1], vb, preferred_element_type=F32)
            z = jnp.where(low_half, za, zb) + bs_ref[:, cols]
            u = u_scr[pl.ds(r0, CHUNK), cols]
            g_scr[pl.ds(r0, CHUNK), cols] = (u * z).astype(BF16)
        return carry

    lax.fori_loop(0, tm // CHUNK, chunk_body, 0, unroll=True)

    y = jnp.dot(g_scr[...], w_out_ref[...], preferred_element_type=F32)
    o_ref[...] = _post_norm(x, y, mod, ln_g_ref[...], ln_b_ref[...])


def _mixer_a(x, mod, w_in, b_in, vn_g, vn_b, w_s, b_s, w_out, ln_g, ln_b):
    batch, seq, _ = x.shape
    tm = 512
    row = lambda a: a.reshape(1, -1)
    bs_plane = jnp.repeat(b_s.T, A_GROUP_DIM, axis=1)
    return pl.pallas_call(
        _mixer_a_kernel,
        grid=(batch, seq // tm),
        in_specs=[
            pl.BlockSpec((None, tm, D_MODEL), lambda b, i: (b, i, 0)),
            pl.BlockSpec((None, 3, D_MODEL), lambda b, i: (b, 0, 0)),
            _resident((D_MODEL, 2 * D_MODEL)),
            _resident((1, 2 * D_MODEL)),
            _resident((1, D_MODEL)),
            _resident((1, D_MODEL)),
            _resident((A_GROUPS, CHUNK, CHUNK)),
            _resident((CHUNK, D_MODEL)),
            _resident((D_MODEL, D_MODEL)),
            _resident((1, D_MODEL)),
            _resident((1, D_MODEL)),
        ],
        out_specs=pl.BlockSpec((None, tm, D_MODEL), lambda b, i: (b, i, 0)),
        out_shape=jax.ShapeDtypeStruct(x.shape, F32),
        scratch_shapes=[
            pltpu.VMEM((A_GROUPS, CHUNK, CHUNK), BF16),
            pltpu.VMEM((tm, D_MODEL), F32),
            pltpu.VMEM((tm, D_MODEL), BF16),
            pltpu.VMEM((tm, D_MODEL), BF16),
        ],
        compiler_params=pltpu.CompilerParams(
            dimension_semantics=("arbitrary", "arbitrary"),
            vmem_limit_bytes=VMEM_LIMIT_BYTES),
        name="mixer_a",
    )(x, mod, w_in.astype(BF16), row(b_in), row(vn_g), row(vn_b), w_s, bs_plane,
      w_out.astype(BF16), row(ln_g), row(ln_b))


MLP_ROW_PARTS = 2


def _mlp_kernel(residue_in, x_ref, mod_ref, w_up_ref, w_down_ref, ln_g_ref, ln_b_ref, o_ref):
    tm = o_ref.shape[0]
    x = x_ref[...].reshape(tm, D_MODEL)
    mod = mod_ref[...]
    h = _modulate(x, mod)
    a = jnp.dot(h, w_up_ref[...], preferred_element_type=F32)
    r = jnp.square(jnp.maximum(a, 0.0)).astype(BF16)
    part = tm // MLP_ROW_PARTS
    outs = []
    for p in range(MLP_ROW_PARTS):
        rows = slice(p * part, (p + 1) * part)
        y = jnp.dot(r[rows], w_down_ref[...], preferred_element_type=F32)
        outs.append(_post_norm(x[rows], y, mod, ln_g_ref[...], ln_b_ref[...]))
    out = jnp.concatenate(outs, axis=0)
    if residue_in:
        out = jnp.swapaxes(out.reshape(RES, tm // RES, D_MODEL), 0, 1).reshape(tm, D_MODEL)
    o_ref[...] = out


def _mlp(x, mod, w_up, w_down, ln_g, ln_b, residue_in):
    batch, seq, _ = x.shape
    tm = 512
    row = lambda a: a.reshape(1, -1)
    nat_spec = pl.BlockSpec((None, tm, D_MODEL), lambda b, i: (b, i, 0))
    res_spec = pl.BlockSpec((None, RES, tm // RES, D_MODEL), lambda b, i: (b, 0, i, 0))
    nat_shape = jax.ShapeDtypeStruct((batch, seq, D_MODEL), F32)
    if residue_in:
        x = x.reshape(batch, RES, seq // RES, D_MODEL)
    return pl.pallas_call(
        functools.partial(_mlp_kernel, residue_in),
        grid=(batch, seq // tm),
        in_specs=[
            res_spec if residue_in else nat_spec,
            pl.BlockSpec((None, 3, D_MODEL), lambda b, i: (b, 0, 0)),
            _resident((D_MODEL, D_FF)),
            _resident((D_FF, D_MODEL)),
            _resident((1, D_MODEL)),
            _resident((1, D_MODEL)),
        ],
        out_specs=nat_spec,
        out_shape=nat_shape,
        compiler_params=pltpu.CompilerParams(
            dimension_semantics=("arbitrary", "arbitrary"),
            vmem_limit_bytes=VMEM_LIMIT_BYTES),
        name="mlp",
    )(x, mod, w_up.astype(BF16), w_down.astype(BF16), row(ln_g), row(ln_b))


LOCAL_KV_BLOCKS = (1, 2)


def _qkv_kernel(x_ref, mod_ref, w_ref, kv_ref, qkv_ref, x_res_ref):
    tm = x_ref.shape[0]
    width = B_HEADS * B_HEAD_DIM
    mod = mod_ref[...]
    x = x_ref[...]
    x_res = jnp.swapaxes(x.reshape(tm // RES, RES, D_MODEL), 0, 1)
    x_res_ref[...] = x_res
    h_nat = _modulate(x, mod)
    for j, c in enumerate(LOCAL_KV_BLOCKS):
        kv_ref[:, j * width:(j + 1) * width] = jnp.dot(
            h_nat, w_ref[:, c * width:(c + 1) * width], preferred_element_type=F32).astype(BF16)
    h_res = _modulate(x_res.reshape(tm, D_MODEL), mod)
    others = [c for c in range(w_ref.shape[1] // width) if c not in LOCAL_KV_BLOCKS]
    for j, c in enumerate(others):
        out = jnp.dot(h_res, w_ref[:, c * width:(c + 1) * width], preferred_element_type=F32)
        qkv_ref[:, :, j * width:(j + 1) * width] = out.astype(BF16).reshape(RES, tm // RES, width)


def _qkv(x, mod, w_qkv):
    batch, seq, _ = x.shape
    n_out = w_qkv.shape[1]
    width = B_HEADS * B_HEAD_DIM
    n_local = len(LOCAL_KV_BLOCKS) * width
    tm = 512
    res_block = lambda w: pl.BlockSpec((None, RES, tm // RES, w), lambda b, i: (b, 0, i, 0))
    res_shape = lambda w, dtype: jax.ShapeDtypeStruct((batch, RES, seq // RES, w), dtype)
    kv, qkv, x_res = pl.pallas_call(
        _qkv_kernel,
        grid=(batch, seq // tm),
        in_specs=[
            pl.BlockSpec((None, tm, D_MODEL), lambda b, i: (b, i, 0)),
            pl.BlockSpec((None, 3, D_MODEL), lambda b, i: (b, 0, 0)),
            _resident((D_MODEL, n_out)),
        ],
        out_specs=[pl.BlockSpec((None, tm, n_local), lambda b, i: (b, i, 0)),
                   res_block(n_out - n_local), res_block(D_MODEL)],
        out_shape=[jax.ShapeDtypeStruct((batch, seq, n_local), BF16),
                   res_shape(n_out - n_local, BF16), res_shape(D_MODEL, F32)],
        compiler_params=pltpu.CompilerParams(
            dimension_semantics=("arbitrary", "arbitrary"),
            vmem_limit_bytes=VMEM_LIMIT_BYTES),
        name="qkv_proj",
    )(x, mod, w_qkv.astype(BF16))
    return kv, qkv.reshape(batch, seq, n_out - n_local), x_res.reshape(batch, seq, D_MODEL)


HALF = SPAN // RES
QUARTER = SPAN // SUB
UNITS = 2 * N_PAT


def _build_bias_tables(bias_scr):
    row = lax.broadcasted_iota(jnp.int32, (SPAN, 2 * SPAN), 0)
    col = lax.broadcasted_iota(jnp.int32, (SPAN, 2 * SPAN), 1)
    diff_local = SPAN + RES * (row % HALF) + row // HALF - col
    diff_mid = (SUB * (row % QUARTER) + row // QUARTER
                - SUB * (col % (2 * QUARTER) - QUARTER) - col // (2 * QUARTER))
    diff_wide = SPAN + row - col
    has_prev = (None, None, None, col >= SPAN, col % (2 * QUARTER) >= QUARTER)
    for t, diff in enumerate((diff_local, diff_mid, diff_wide, diff_local, diff_mid)):
        dilation = B_PATTERNS[t % N_PAT][1]
        valid = jnp.logical_and(diff >= 0, diff <= SPAN)
        if has_prev[t] is not None:
            valid = jnp.logical_and(valid, has_prev[t])
        dist = (dilation * diff).astype(F32)
        for hd in range(B_HEADS):
            slope = 2.0 ** (-8.0 * (hd + 1) / B_HEADS)
            bias_scr[t, hd] = jnp.where(valid, -slope * dist, NEG)


def _attn_kernel(q1_ref, k1p_ref, k1c_ref, v1p_ref, v1c_ref,
                 q2_ref, k2p_ref, k2c_ref, v2p_ref, v2c_ref,
                 q3_ref, k3_ref, v3_ref,
                 o1_ref, l1_ref, o2_ref, l2_ref, o3_ref, l3_ref, bias_scr, m_scr, l_scr):
    step = pl.program_id(1)

    @pl.when(jnp.logical_and(pl.program_id(0) == 0, step == 0))
    def _():
        _build_bias_tables(bias_scr)
        m_scr[...] = jnp.zeros(m_scr.shape, F32)
        l_scr[...] = jnp.ones(l_scr.shape, F32)

    lane = lax.broadcasted_iota(jnp.int32, (SPAN, LANES), 1)
    low_half = lane < B_HEAD_DIM
    scale = B_HEAD_DIM ** -0.5
    q_mask = (jnp.where(low_half, scale, 0.0).astype(BF16),
              jnp.where(low_half, 0.0, scale).astype(BF16))

    mid_steps = pl.num_programs(1) // MID
    table_local_a = jnp.where(step == 0, N_PAT, 0)
    table_mid_a = jnp.where(step % mid_steps == 0, N_PAT + 1, 1)

    def unit(q2, k2, v2, table, hp, own_block_only=False):
        q_both = jnp.concatenate([q2 * q_mask[0], q2 * q_mask[1]], axis=0)
        s_both = lax.dot_general(q_both, k2, (((1,), (1,)), ((), ())),
                                 preferred_element_type=F32)
        ps, ms, ls = [], [], []
        for j in range(2):
            hd = 2 * hp + j
            s = s_both[j * SPAN:(j + 1) * SPAN]
            if own_block_only:
                s = s + bias_scr[table, hd, :, SPAN:]
            else:
                s = s + bias_scr[table, hd]
            m = jnp.max(s, axis=-1, keepdims=True)
            p = jnp.exp(s - m)
            ls.append(jnp.sum(p, axis=-1, keepdims=True))
            ms.append(m)
            ps.append(p.astype(BF16))
        pv = jnp.dot(jnp.concatenate(ps, axis=0), v2, preferred_element_type=F32)
        o = jnp.where(low_half, pv[:SPAN] / ls[0], pv[SPAN:] / ls[1])
        return o, (ms, ls)

    def put(u, hp, stats):
        for j in range(2):
            hd = 2 * hp + j
            m_scr[u, :, hd:hd + 1] = stats[0][j]
            l_scr[u, :, hd:hd + 1] = stats[1][j]

    for hp in range(B_HEADS // 2):
        cols = slice(hp * LANES, (hp + 1) * LANES)

        q1 = q1_ref[:, :, cols].astype(F32).reshape(RES, 2, HALF, LANES)
        q1a = q1[:, 0].reshape(SPAN, LANES).astype(BF16)
        q1b = q1[:, 1].reshape(SPAN, LANES).astype(BF16)
        k1c = k1c_ref[:, cols]
        v1c = v1c_ref[:, cols]
        oa, sa = unit(q1a, jnp.concatenate([k1p_ref[:, cols], k1c[:SPAN]], axis=0),
                      jnp.concatenate([v1p_ref[:, cols], v1c[:SPAN]], axis=0),
                      table_local_a, hp)
        ob, sb = unit(q1b, k1c, v1c, 0, hp)
        o1 = jnp.concatenate([oa.reshape(RES, HALF, LANES), ob.reshape(RES, HALF, LANES)], axis=1)
        o1_ref[:, :, cols] = o1.astype(BF16)
        put(0, hp, sa)
        put(1, hp, sb)

        q2 = q2_ref[:, :, cols]
        k2p, k2c = k2p_ref[:, :, cols], k2c_ref[:, :, cols]
        v2p, v2c = v2p_ref[:, :, cols], v2c_ref[:, :, cols]
        q2a = q2[:, :QUARTER].reshape(SPAN, LANES)
        q2b = q2[:, QUARTER:].reshape(SPAN, LANES)
        k2a = jnp.concatenate([k2p, k2c[:, :QUARTER]], axis=1).reshape(2 * SPAN, LANES)
        v2a = jnp.concatenate([v2p, v2c[:, :QUARTER]], axis=1).reshape(2 * SPAN, LANES)
        oa, sa = unit(q2a, k2a, v2a, table_mid_a, hp)
        ob, sb = unit(q2b, k2c.reshape(2 * SPAN, LANES), v2c.reshape(2 * SPAN, LANES), 1, hp)
        o2_ref[:, :QUARTER, cols] = oa.reshape(SUB, QUARTER, LANES).astype(BF16)
        o2_ref[:, QUARTER:, cols] = ob.reshape(SUB, QUARTER, LANES).astype(BF16)
        put(2, hp, sa)
        put(3, hp, sb)

        q3, k3, v3 = q3_ref[:, cols], k3_ref[:, cols], v3_ref[:, cols]
        oa, sa = unit(q3[:SPAN], k3[:SPAN], v3[:SPAN], 2, hp, own_block_only=True)
        ob, sb = unit(q3[SPAN:], k3, v3, 2, hp)
        o3_ref[:SPAN, cols] = oa.astype(BF16)
        o3_ref[SPAN:, cols] = ob.astype(BF16)
        put(4, hp, sa)
        put(5, hp, sb)

    l1a, l1b, l2a, l2b, l3a, l3b = [m_scr[u] + jnp.log(l_scr[u]) for u in range(UNITS)]
    l1_ref[...] = jnp.concatenate(
        [l1a.reshape(RES, HALF, LANES), l1b.reshape(RES, HALF, LANES)], axis=1)
    l2_ref[:, :QUARTER] = l2a.reshape(SUB, QUARTER, LANES)
    l2_ref[:, QUARTER:] = l2b.reshape(SUB, QUARTER, LANES)
    l3_ref[:SPAN] = l3a
    l3_ref[SPAN:] = l3b


def _attention(kv_local, qkv):
    batch, seq, n_qkv = qkv.shape
    width = B_HEADS * B_HEAD_DIM
    per = seq // RES
    n_steps = seq // (2 * SPAN)
    by_res = qkv.reshape(batch, RES, per, n_qkv)
    by_sub = qkv.reshape(batch, SUB, MID, per, n_qkv)

    def col(g, part):
        return 0 if g == 0 else 3 * g - 2 + part

    mid_steps = n_steps // MID

    def mid_spec(rows, part, prev):
        def index_map(b, s):
            blk = s % mid_steps
            if prev:
                blk = jnp.maximum(2 * blk - 1, 0)
            return (b, 0, s // mid_steps, blk, col(1, part))
        return pl.BlockSpec((None, SUB, None, rows, width), index_map)

    in_specs = [
        pl.BlockSpec((None, RES, 2 * HALF, width), lambda b, s: (b, 0, s, col(0, 0))),
        pl.BlockSpec((None, SPAN, width), lambda b, s: (b, jnp.maximum(2 * s - 1, 0), 0)),
        pl.BlockSpec((None, 2 * SPAN, width), lambda b, s: (b, s, 0)),
        pl.BlockSpec((None, SPAN, width), lambda b, s: (b, jnp.maximum(2 * s - 1, 0), 1)),
        pl.BlockSpec((None, 2 * SPAN, width), lambda b, s: (b, s, 1)),
        mid_spec(2 * QUARTER, 0, False),
        mid_spec(QUARTER, 1, True), mid_spec(2 * QUARTER, 1, False),
        mid_spec(QUARTER, 2, True), mid_spec(2 * QUARTER, 2, False),
        pl.BlockSpec((None, None, per, width), lambda b, s: (b, s, 0, col(2, 0))),
        pl.BlockSpec((None, None, per, width), lambda b, s: (b, s, 0, col(2, 1))),
        pl.BlockSpec((None, None, per, width), lambda b, s: (b, s, 0, col(2, 2))),
    ]
    operands = [by_res] + [kv_local] * 4 + [by_sub] * 5 + [by_res] * 3

    out_specs, out_shapes = [], []

    def outs_for(w, dtype):
        return (
            [pl.BlockSpec((None, RES, 2 * HALF, w), lambda b, s: (b, 0, s, 0)),
             pl.BlockSpec((None, SUB, None, 2 * QUARTER, w),
                          lambda b, s: (b, 0, s // mid_steps, s % mid_steps, 0)),
             pl.BlockSpec((None, None, per, w), lambda b, s: (b, s, 0, 0))],
            [jax.ShapeDtypeStruct((batch, RES, per, w), dtype),
             jax.ShapeDtypeStruct((batch, SUB, MID, per, w), dtype),
             jax.ShapeDtypeStruct((batch, RES, per, w), dtype)])
    o_specs, o_shapes = outs_for(width, BF16)
    l_specs, l_shapes = outs_for(LANES, F32)
    for g in range(N_PAT):
        out_specs += [o_specs[g], l_specs[g]]
        out_shapes += [o_shapes[g], l_shapes[g]]

    outs = pl.pallas_call(
        _attn_kernel,
        grid=(batch, n_steps),
        in_specs=in_specs,
        out_specs=out_specs,
        out_shape=out_shapes,
        scratch_shapes=[pltpu.VMEM((N_PAT + 2, B_HEADS, SPAN, 2 * SPAN), F32),
                        pltpu.VMEM((UNITS, SPAN, LANES), F32),
                        pltpu.VMEM((UNITS, SPAN, LANES), F32)],
        compiler_params=pltpu.CompilerParams(
            dimension_semantics=("arbitrary", "arbitrary"),
            vmem_limit_bytes=VMEM_LIMIT_BYTES),
        name="dilated_attn",
    )(*operands)
    o = [outs[2 * g].reshape(batch, seq, width) for g in range(N_PAT)]
    lse = [outs[2 * g + 1].reshape(batch, seq, LANES) for g in range(N_PAT)]
    return o, lse


def _attn_out_kernel(o0_ref, o1_ref, o2_ref, l0_ref, l1_ref, l2_ref, x_ref, mod_ref, w_out_ref,
                     ln_g_ref, ln_b_ref, out_ref):
    lses = [l0_ref[...], l1_ref[...], l2_ref[...]]
    mx = jnp.maximum(jnp.maximum(lses[0], lses[1]), lses[2])
    es = [jnp.exp(l - mx) for l in lses]
    den = es[0] + es[1] + es[2]
    src_lane = lax.broadcasted_iota(jnp.int32, (LANES, D_MODEL), 0)
    head_of_col = lax.broadcasted_iota(jnp.int32, (LANES, D_MODEL), 1) // B_HEAD_DIM
    expand = jnp.logical_and(src_lane < 2 * B_HEADS, src_lane % B_HEADS == head_of_col)
    expand = expand.astype(BF16)
    lane = lax.broadcasted_iota(jnp.int32, lses[0].shape, 1)
    o = o2_ref[...].astype(F32)
    base = o
    for e, o_ref in zip(es[:2], (o0_ref, o1_ref)):
        wts = e / den
        head = wts.astype(BF16).astype(F32)
        rest = pltpu.roll(wts - head, B_HEADS, axis=1)
        split = jnp.where(lane < B_HEADS, head, jnp.where(lane < 2 * B_HEADS, rest, 0.0))
        w = jnp.dot(split.astype(BF16), expand, preferred_element_type=F32)
        o = o + w * (o_ref[...].astype(F32) - base)
    y = jnp.dot(o.astype(BF16), w_out_ref[...], preferred_element_type=F32)
    mod = mod_ref[...]
    out_ref[...] = _post_norm(x_ref[...], y, mod, ln_g_ref[...], ln_b_ref[...])


def _attn_out(o, lse, x, mod, w_out, ln_g, ln_b):
    batch, seq, _ = x.shape
    tm = 512
    row = lambda a: a.reshape(1, -1)
    tile = lambda w: pl.BlockSpec((None, tm, w), lambda b, i: (b, i, 0))
    return pl.pallas_call(
        _attn_out_kernel,
        grid=(batch, seq // tm),
        in_specs=[tile(D_MODEL)] * 3 + [tile(LANES)] * 3 + [
            tile(D_MODEL),
            pl.BlockSpec((None, 3, D_MODEL), lambda b, i: (b, 0, 0)),
            _resident((D_MODEL, D_MODEL)),
            _resident((1, D_MODEL)),
            _resident((1, D_MODEL)),
        ],
        out_specs=tile(D_MODEL),
        out_shape=jax.ShapeDtypeStruct(x.shape, F32),
        compiler_params=pltpu.CompilerParams(
            dimension_semantics=("arbitrary", "arbitrary"),
            vmem_limit_bytes=VMEM_LIMIT_BYTES),
        name="attn_out",
    )(*o, *lse, x, mod, w_out.astype(BF16), row(ln_g), row(ln_b))


def kernel(x, c, ada_w, ada_b, ln_g, ln_b, a_w_in, a_b_in, a_vn_g, a_vn_b, a_w_s, a_b_s, a_w_out,
           b_w_qkv, b_w_out, mlp_w_up, mlp_w_down):
    mods = _ada_mod(c, ada_w, ada_b)
    x = _mixer_a(x, mods[0], a_w_in[0], a_b_in[0], a_vn_g[0], a_vn_b[0], a_w_s[0], a_b_s[0],
                 a_w_out[0], ln_g[0, 0], ln_b[0, 0])
    x = _mlp(x, mods[1], mlp_w_up[0], mlp_w_down[0], ln_g[0, 1], ln_b[0, 1], residue_in=False)
    kv_local, qkv, x_res = _qkv(x, mods[2], b_w_qkv[0])
    o, lse = _attention(kv_local, qkv)
    x_res = _attn_out(o, lse, x_res, mods[2], b_w_out[0], ln_g[1, 0], ln_b[1, 0])
    return _mlp(x_res, mods[3], mlp_w_up[1], mlp_w_down[1], ln_g[1, 1], ln_b[1, 1],
                residue_in=True)
```

```python
import functools

import jax
import jax.numpy as jnp
from jax import lax
from jax.experimental import pallas as pl
from jax.experimental.pallas import tpu as pltpu

D_MODEL = 1024
DEPTH = 2
CHUNK = 128
A_GROUPS = 16
A_GROUP_DIM = D_MODEL // A_GROUPS
B_HEADS = 16
B_HEAD_DIM = D_MODEL // B_HEADS
B_PATTERNS = ((128, 1), (512, 4), (2048, 16))
N_PAT = len(B_PATTERNS)
SPAN = 128
D_FF = 4 * D_MODEL
ALPHA = (2 * DEPTH) ** 0.25
LN_EPS = 1e-5
NEG = -1e30

LANES = 128
SUBLANES = 8
VMEM_LIMIT_BYTES = 56 * 1024 * 1024
SINGLE_BUFFER_MIN_ELEMENTS = 1024 * 1024

RES = 16
MID = 4
SUB = RES // MID

F32 = jnp.float32
BF16 = jnp.bfloat16


def _layer_norm(x, g, b):
    mu = jnp.mean(x, axis=-1, keepdims=True)
    xc = x - mu
    var = jnp.mean(xc * xc, axis=-1, keepdims=True)
    return xc * lax.rsqrt(var + LN_EPS) * g + b


def _resident(shape):
    zeros = (0,) * len(shape)
    elements = 1
    for dim in shape:
        elements *= dim
    if elements < SINGLE_BUFFER_MIN_ELEMENTS:
        return pl.BlockSpec(shape, lambda *_: zeros)
    return pl.BlockSpec(shape, lambda *_: zeros, pipeline_mode=pl.Buffered(1))


def _ada_kernel(c_ref, w_ref, b_ref, o_ref):
    s = jax.nn.silu(c_ref[...]).astype(BF16)
    o_ref[0] = jnp.dot(s, w_ref[0].astype(BF16), preferred_element_type=F32) + b_ref[0]


def _ada_mod(c, ada_w, ada_b):
    batch = c.shape[0]
    rows = SUBLANES
    c_pad = jnp.zeros((rows, D_MODEL), F32).at[:batch].set(c)
    n_sub = DEPTH * 2
    w = ada_w.reshape(n_sub, D_MODEL, 3 * D_MODEL)
    b = ada_b.reshape(n_sub, 1, 3 * D_MODEL)
    tn = 1024
    out = pl.pallas_call(
        _ada_kernel,
        grid=(n_sub, 3 * D_MODEL // tn),
        in_specs=[
            pl.BlockSpec((rows, D_MODEL), lambda i, j: (0, 0)),
            pl.BlockSpec((1, D_MODEL, tn), lambda i, j: (i, 0, j)),
            pl.BlockSpec((1, 1, tn), lambda i, j: (i, 0, j)),
        ],
        out_specs=pl.BlockSpec((1, rows, tn), lambda i, j: (i, 0, j)),
        out_shape=jax.ShapeDtypeStruct((n_sub, rows, 3 * D_MODEL), F32),
        name="ada_mod",
    )(c_pad, w, b)
    return out[:, :batch].reshape(n_sub, batch, 3, D_MODEL)


def _modulate(x, mod):
    shift = mod[0:1]
    scale = mod[1:2]
    return (x * (1.0 + scale) + shift).astype(BF16)


def _post_norm(x, y, mod, g, b):
    gate = 1.0 + mod[2:3]
    return _layer_norm(ALPHA * x + gate * y, g, b)


def _mixer_a_kernel(x_ref, mod_ref, w_in_ref, b_in_ref, vn_g_ref, vn_b_ref, ws_ref, bs_ref,
                    w_out_ref, ln_g_ref, ln_b_ref, o_ref, ws_scr, u_scr, v_scr, g_scr):
    tm = x_ref.shape[0]

    @pl.when(jnp.logical_and(pl.program_id(0) == 0, pl.program_id(1) == 0))
    def _():
        t = lax.broadcasted_iota(jnp.int32, (CHUNK, CHUNK), 0)
        s = lax.broadcasted_iota(jnp.int32, (CHUNK, CHUNK), 1)
        causal = s <= t
        for g in range(A_GROUPS):
            ws_scr[g] = jnp.where(causal, ws_ref[g], 0.0).astype(BF16)

    x = x_ref[...]
    mod = mod_ref[...]
    h = _modulate(x, mod)
    v = jnp.dot(h, w_in_ref[:, D_MODEL:], preferred_element_type=F32) + b_in_ref[:, D_MODEL:]
    u = jnp.dot(h, w_in_ref[:, :D_MODEL], preferred_element_type=F32) + b_in_ref[:, :D_MODEL]
    v = _layer_norm(jax.nn.gelu(v), vn_g_ref[...], vn_b_ref[...])
    v_scr[...] = v.astype(BF16)
    u_scr[...] = jax.nn.gelu(u)

    low_half = lax.broadcasted_iota(jnp.int32, (CHUNK, LANES), 1) < A_GROUP_DIM

    def chunk_body(c, carry):
        r0 = pl.multiple_of(c * CHUNK, CHUNK)
        for gp in range(A_GROUPS // 2):
            cols = slice(gp * LANES, (gp + 1) * LANES)
            vb = v_scr[pl.ds(r0, CHUNK), cols]
            za = jnp.dot(ws_scr[2 * gp], vb, preferred_element_type=F32)
            zb = jnp.dot(ws_scr[2 * gp + 1], vb, preferred_element_type=F32)
            z = jnp.where(low_half, za, zb) + bs_ref[:, cols]
            u = u_scr[pl.ds(r0, CHUNK), cols]
            g_scr[pl.ds(r0, CHUNK), cols] = (u * z).astype(BF16)
        return carry

    lax.fori_loop(0, tm // CHUNK, chunk_body, 0, unroll=True)

    y = jnp.dot(g_scr[...], w_out_ref[...], preferred_element_type=F32)
    o_ref[...] = _post_norm(x, y, mod, ln_g_ref[...], ln_b_ref[...])


def _mixer_a(x, mod, w_in, b_in, vn_g, vn_b, w_s, b_s, w_out, ln_g, ln_b):
    batch, seq, _ = x.shape
    tm = 512
    row = lambda a: a.reshape(1, -1)
    bs_plane = jnp.repeat(b_s.T, A_GROUP_DIM, axis=1)
    return pl.pallas_call(
        _mixer_a_kernel,
        grid=(batch, seq // tm),
        in_specs=[
            pl.BlockSpec((None, tm, D_MODEL), lambda b, i: (b, i, 0)),
            pl.BlockSpec((None, 3, D_MODEL), lambda b, i: (b, 0, 0)),
            _resident((D_MODEL, 2 * D_MODEL)),
            _resident((1, 2 * D_MODEL)),
            _resident((1, D_MODEL)),
            _resident((1, D_MODEL)),
            _resident((A_GROUPS, CHUNK, CHUNK)),
            _resident((CHUNK, D_MODEL)),
            _resident((D_MODEL, D_MODEL)),
            _resident((1, D_MODEL)),
            _resident((1, D_MODEL)),
        ],
        out_specs=pl.BlockSpec((None, tm, D_MODEL), lambda b, i: (b, i, 0)),
        out_shape=jax.ShapeDtypeStruct(x.shape, F32),
        scratch_shapes=[
            pltpu.VMEM((A_GROUPS, CHUNK, CHUNK), BF16),
            pltpu.VMEM((tm, D_MODEL), F32),
            pltpu.VMEM((tm, D_MODEL), BF16),
            pltpu.VMEM((tm, D_MODEL), BF16),
        ],
        compiler_params=pltpu.CompilerParams(
            dimension_semantics=("arbitrary", "arbitrary"),
            vmem_limit_bytes=VMEM_LIMIT_BYTES),
        name="mixer_a",
    )(x, mod, w_in.astype(BF16), row(b_in), row(vn_g), row(vn_b), w_s, bs_plane,
      w_out.astype(BF16), row(ln_g), row(ln_b))


MLP_ROW_PARTS = 2


def _mlp_kernel(residue_in, x_ref, mod_ref, w_up_ref, w_down_ref, ln_g_ref, ln_b_ref, o_ref):
    tm = o_ref.shape[0]
    x = x_ref[...].reshape(tm, D_MODEL)
    mod = mod_ref[...]
    h = _modulate(x, mod)
    a = jnp.dot(h, w_up_ref[...], preferred_element_type=F32)
    r = jnp.square(jnp.maximum(a, 0.0)).astype(BF16)
    part = tm // MLP_ROW_PARTS
    outs = []
    for p in range(MLP_ROW_PARTS):
        rows = slice(p * part, (p + 1) * part)
        y = jnp.dot(r[rows], w_down_ref[...], preferred_element_type=F32)
        outs.append(_post_norm(x[rows], y, mod, ln_g_ref[...], ln_b_ref[...]))
    out = jnp.concatenate(outs, axis=0)
    if residue_in:
        out = jnp.swapaxes(out.reshape(RES, tm // RES, D_MODEL), 0, 1).reshape(tm, D_MODEL)
    o_ref[...] = out


def _mlp(x, mod, w_up, w_down, ln_g, ln_b, residue_in):
    batch, seq, _ = x.shape
    tm = 512
    row = lambda a: a.reshape(1, -1)
    nat_spec = pl.BlockSpec((None, tm, D_MODEL), lambda b, i: (b, i, 0))
    res_spec = pl.BlockSpec((None, RES, tm // RES, D_MODEL), lambda b, i: (b, 0, i, 0))
    nat_shape = jax.ShapeDtypeStruct((batch, seq, D_MODEL), F32)
    if residue_in:
        x = x.reshape(batch, RES, seq // RES, D_MODEL)
    return pl.pallas_call(
        functools.partial(_mlp_kernel, residue_in),
        grid=(batch, seq // tm),
        in_specs=[
            res_spec if residue_in else nat_spec,
            pl.BlockSpec((None, 3, D_MODEL), lambda b, i: (b, 0, 0)),
            _resident((D_MODEL, D_FF)),
            _resident((D_FF, D_MODEL)),
            _resident((1, D_MODEL)),
            _resident((1, D_MODEL)),
        ],
        out_specs=nat_spec,
        out_shape=nat_shape,
        compiler_params=pltpu.CompilerParams(
            dimension_semantics=("arbitrary", "arbitrary"),
            vmem_limit_bytes=VMEM_LIMIT_BYTES),
        name="mlp",
    )(x, mod, w_up.astype(BF16), w_down.astype(BF16), row(ln_g), row(ln_b))


LOCAL_KV_BLOCKS = (1, 2)


def _qkv_kernel(x_ref, mod_ref, w_ref, kv_ref, qkv_ref, x_res_ref):
    tm = x_ref.shape[0]
    width = B_HEADS * B_HEAD_DIM
    mod = mod_ref[...]
    x = x_ref[...]
    x_res = jnp.swapaxes(x.reshape(tm // RES, RES, D_MODEL), 0, 1)
    x_res_ref[...] = x_res
    h_nat = _modulate(x, mod)
    for j, c in enumerate(LOCAL_KV_BLOCKS):
        kv_ref[:, j * width:(j + 1) * width] = jnp.dot(
            h_nat, w_ref[:, c * width:(c + 1) * width], preferred_element_type=F32).astype(BF16)
    h_res = _modulate(x_res.reshape(tm, D_MODEL), mod)
    others = [c for c in range(w_ref.shape[1] // width) if c not in LOCAL_KV_BLOCKS]
    for j, c in enumerate(others):
        out = jnp.dot(h_res, w_ref[:, c * width:(c + 1) * width], preferred_element_type=F32)
        qkv_ref[:, :, j * width:(j + 1) * width] = out.astype(BF16).reshape(RES, tm // RES, width)


def _qkv(x, mod, w_qkv):
    batch, seq, _ = x.shape
    n_out = w_qkv.shape[1]
    width = B_HEADS * B_HEAD_DIM
    n_local = len(LOCAL_KV_BLOCKS) * width
    tm = 512
    res_block = lambda w: pl.BlockSpec((None, RES, tm // RES, w), lambda b, i: (b, 0, i, 0))
    res_shape = lambda w, dtype: jax.ShapeDtypeStruct((batch, RES, seq // RES, w), dtype)
    kv, qkv, x_res = pl.pallas_call(
        _qkv_kernel,
        grid=(batch, seq // tm),
        in_specs=[
            pl.BlockSpec((None, tm, D_MODEL), lambda b, i: (b, i, 0)),
            pl.BlockSpec((None, 3, D_MODEL), lambda b, i: (b, 0, 0)),
            _resident((D_MODEL, n_out)),
        ],
        out_specs=[pl.BlockSpec((None, tm, n_local), lambda b, i: (b, i, 0)),
                   res_block(n_out - n_local), res_block(D_MODEL)],
        out_shape=[jax.ShapeDtypeStruct((batch, seq, n_local), BF16),
                   res_shape(n_out - n_local, BF16), res_shape(D_MODEL, F32)],
        compiler_params=pltpu.CompilerParams(
            dimension_semantics=("arbitrary", "arbitrary"),
            vmem_limit_bytes=VMEM_LIMIT_BYTES),
        name="qkv_proj",
    )(x, mod, w_qkv.astype(BF16))
    return kv, qkv.reshape(batch, seq, n_out - n_local), x_res.reshape(batch, seq, D_MODEL)


HALF = SPAN // RES
QUARTER = SPAN // SUB
UNITS = 2 * N_PAT


def _build_bias_tables(bias_scr):
    row = lax.broadcasted_iota(jnp.int32, (SPAN, 2 * SPAN), 0)
    col = lax.broadcasted_iota(jnp.int32, (SPAN, 2 * SPAN), 1)
    diff_local = SPAN + RES * (row % HALF) + row // HALF - col
    diff_mid = (SUB * (row % QUARTER) + row // QUARTER
                - SUB * (col % (2 * QUARTER) - QUARTER) - col // (2 * QUARTER))
    diff_wide = SPAN + row - col
    has_prev = (None, None, None, col >= SPAN, col % (2 * QUARTER) >= QUARTER)
    for t, diff in enumerate((diff_local, diff_mid, diff_wide, diff_local, diff_mid)):
        dilation = B_PATTERNS[t % N_PAT][1]
        valid = jnp.logical_and(diff >= 0, diff <= SPAN)
        if has_prev[t] is not None:
            valid = jnp.logical_and(valid, has_prev[t])
        dist = (dilation * diff).astype(F32)
        for hd in range(B_HEADS):
            slope = 2.0 ** (-8.0 * (hd + 1) / B_HEADS)
            bias_scr[t, hd] = jnp.where(valid, -slope * dist, NEG)


def _attn_kernel(q1_ref, k1p_ref, k1c_ref, v1p_ref, v1c_ref,
                 q2_ref, k2p_ref, k2c_ref, v2p_ref, v2c_ref,
                 q3_ref, k3_ref, v3_ref,
                 o1_ref, l1_ref, o2_ref, l2_ref, o3_ref, l3_ref, bias_scr, m_scr, l_scr):
    step = pl.program_id(1)

    @pl.when(jnp.logical_and(pl.program_id(0) == 0, step == 0))
    def _():
        _build_bias_tables(bias_scr)
        m_scr[...] = jnp.zeros(m_scr.shape, F32)
        l_scr[...] = jnp.ones(l_scr.shape, F32)

    lane = lax.broadcasted_iota(jnp.int32, (SPAN, LANES), 1)
    low_half = lane < B_HEAD_DIM
    scale = B_HEAD_DIM ** -0.5
    q_mask = (jnp.where(low_half, scale, 0.0).astype(BF16),
              jnp.where(low_half, 0.0, scale).astype(BF16))

    mid_steps = pl.num_programs(1) // MID
    table_local_a = jnp.where(step == 0, N_PAT, 0)
    table_mid_a = jnp.where(step % mid_steps == 0, N_PAT + 1, 1)

    def unit(q2, k2, v2, table, hp, own_block_only=False):
        q_both = jnp.concatenate([q2 * q_mask[0], q2 * q_mask[1]], axis=0)
        s_both = lax.dot_general(q_both, k2, (((1,), (1,)), ((), ())),
                                 preferred_element_type=F32)
        ps, ms, ls = [], [], []
        for j in range(2):
            hd = 2 * hp + j
            s = s_both[j * SPAN:(j + 1) * SPAN]
            if own_block_only:
                s = s + bias_scr[table, hd, :, SPAN:]
            else:
                s = s + bias_scr[table, hd]
            m = jnp.max(s, axis=-1, keepdims=True)
            p = jnp.exp(s - m)
            ls.append(jnp.sum(p, axis=-1, keepdims=True))
            ms.append(m)
            ps.append(p.astype(BF16))
        pv = jnp.dot(jnp.concatenate(ps, axis=0), v2, preferred_element_type=F32)
        o = jnp.where(low_half, pv[:SPAN] / ls[0], pv[SPAN:] / ls[1])
        return o, (ms, ls)

    def put(u, hp, stats):
        for j in range(2):
            hd = 2 * hp + j
            m_scr[u, :, hd:hd + 1] = stats[0][j]
            l_scr[u, :, hd:hd + 1] = stats[1][j]

    for hp in range(B_HEADS // 2):
        cols = slice(hp * LANES, (hp + 1) * LANES)

        q1 = q1_ref[:, :, cols].astype(F32).reshape(RES, 2, HALF, LANES)
        q1a = q1[:, 0].reshape(SPAN, LANES).astype(BF16)
        q1b = q1[:, 1].reshape(SPAN, LANES).astype(BF16)
        k1c = k1c_ref[:, cols]
        v1c = v1c_ref[:, cols]
        oa, sa = unit(q1a, jnp.concatenate([k1p_ref[:, cols], k1c[:SPAN]], axis=0),
                      jnp.concatenate([v1p_ref[:, cols], v1c[:SPAN]], axis=0),
                      table_local_a, hp)
        ob, sb = unit(q1b, k1c, v1c, 0, hp)
        o1 = jnp.concatenate([oa.reshape(RES, HALF, LANES), ob.reshape(RES, HALF, LANES)], axis=1)
        o1_ref[:, :, cols] = o1.astype(BF16)
        put(0, hp, sa)
        put(1, hp, sb)

        q2 = q2_ref[:, :, cols]
        k2p, k2c = k2p_ref[:, :, cols], k2c_ref[:, :, cols]
        v2p, v2c = v2p_ref[:, :, cols], v2c_ref[:, :, cols]
        q2a = q2[:, :QUARTER].reshape(SPAN, LANES)
        q2b = q2[:, QUARTER:].reshape(SPAN, LANES)
        k2a = jnp.concatenate([k2p, k2c[:, :QUARTER]], axis=1).reshape(2 * SPAN, LANES)
        v2a = jnp.concatenate([v2p, v2c[:, :QUARTER]], axis=1).reshape(2 * SPAN, LANES)
        oa, sa = unit(q2a, k2a, v2a, table_mid_a, hp)
        ob, sb = unit(q2b, k2c.reshape(2 * SPAN, LANES), v2c.reshape(2 * SPAN, LANES), 1, hp)
        o2_ref[:, :QUARTER, cols] = oa.reshape(SUB, QUARTER, LANES).astype(BF16)
        o2_ref[:, QUARTER:, cols] = ob.reshape(SUB, QUARTER, LANES).astype(BF16)
        put(2, hp, sa)
        put(3, hp, sb)

        q3, k3, v3 = q3_ref[:, cols], k3_ref[:, cols], v3_ref[:, cols]
        oa, sa = unit(q3[:SPAN], k3[:SPAN], v3[:SPAN], 2, hp, own_block_only=True)
        ob, sb = unit(q3[SPAN:], k3, v3, 2, hp)
        o3_ref[:SPAN, cols] = oa.astype(BF16)
        o3_ref[SPAN:, cols] = ob.astype(BF16)
        put(4, hp, sa)
        put(5, hp, sb)

    l1a, l1b, l2a, l2b, l3a, l3b = [m_scr[u] + jnp.log(l_scr[u]) for u in range(UNITS)]
    l1_ref[...] = jnp.concatenate(
        [l1a.reshape(RES, HALF, LANES), l1b.reshape(RES, HALF, LANES)], axis=1)
    l2_ref[:, :QUARTER] = l2a.reshape(SUB, QUARTER, LANES)
    l2_ref[:, QUARTER:] = l2b.reshape(SUB, QUARTER, LANES)
    l3_ref[:SPAN] = l3a
    l3_ref[SPAN:] = l3b


def _attention(kv_local, qkv):
    batch, seq, n_qkv = qkv.shape
    width = B_HEADS * B_HEAD_DIM
    per = seq // RES
    n_steps = seq // (2 * SPAN)
    by_res = qkv.reshape(batch, RES, per, n_qkv)
    by_sub = qkv.reshape(batch, SUB, MID, per, n_qkv)

    def col(g, part):
        return 0 if g == 0 else 3 * g - 2 + part

    mid_steps = n_steps // MID

    def mid_spec(rows, part, prev):
        def index_map(b, s):
            blk = s % mid_steps
            if prev:
                blk = jnp.maximum(2 * blk - 1, 0)
            return (b, 0, s // mid_steps, blk, col(1, part))
        return pl.BlockSpec((None, SUB, None, rows, width), index_map)

    in_specs = [
        pl.BlockSpec((None, RES, 2 * HALF, width), lambda b, s: (b, 0, s, col(0, 0))),
        pl.BlockSpec((None, SPAN, width), lambda b, s: (b, jnp.maximum(2 * s - 1, 0), 0)),
        pl.BlockSpec((None, 2 * SPAN, width), lambda b, s: (b, s, 0)),
        pl.BlockSpec((None, SPAN, width), lambda b, s: (b, jnp.maximum(2 * s - 1, 0), 1)),
        pl.BlockSpec((None, 2 * SPAN, width), lambda b, s: (b, s, 1)),
        mid_spec(2 * QUARTER, 0, False),
        mid_spec(QUARTER, 1, True), mid_spec(2 * QUARTER, 1, False),
        mid_spec(QUARTER, 2, True), mid_spec(2 * QUARTER, 2, False),
        pl.BlockSpec((None, None, per, width), lambda b, s: (b, s, 0, col(2, 0))),
        pl.BlockSpec((None, None, per, width), lambda b, s: (b, s, 0, col(2, 1))),
        pl.BlockSpec((None, None, per, width), lambda b, s: (b, s, 0, col(2, 2))),
    ]
    operands = [by_res] + [kv_local] * 4 + [by_sub] * 5 + [by_res] * 3

    out_specs, out_shapes = [], []

    def outs_for(w, dtype):
        return (
            [pl.BlockSpec((None, RES, 2 * HALF, w), lambda b, s: (b, 0, s, 0)),
             pl.BlockSpec((None, SUB, None, 2 * QUARTER, w),
                          lambda b, s: (b, 0, s // mid_steps, s % mid_steps, 0)),
             pl.BlockSpec((None, None, per, w), lambda b, s: (b, s, 0, 0))],
            [jax.ShapeDtypeStruct((batch, RES, per, w), dtype),
             jax.ShapeDtypeStruct((batch, SUB, MID, per, w), dtype),
             jax.ShapeDtypeStruct((batch, RES, per, w), dtype)])
    o_specs, o_shapes = outs_for(width, BF16)
    l_specs, l_shapes = outs_for(LANES, F32)
    for g in range(N_PAT):
        out_specs += [o_specs[g], l_specs[g]]
        out_shapes += [o_shapes[g], l_shapes[g]]

    outs = pl.pallas_call(
        _attn_kernel,
        grid=(batch, n_steps),
        in_specs=in_specs,
        out_specs=out_specs,
        out_shape=out_shapes,
        scratch_shapes=[pltpu.VMEM((N_PAT + 2, B_HEADS, SPAN, 2 * SPAN), F32),
                        pltpu.VMEM((UNITS, SPAN, LANES), F32),
                        pltpu.VMEM((UNITS, SPAN, LANES), F32)],
        compiler_params=pltpu.CompilerParams(
            dimension_semantics=("arbitrary", "arbitrary"),
            vmem_limit_bytes=VMEM_LIMIT_BYTES),
        name="dilated_attn",
    )(*operands)
    o = [outs[2 * g].reshape(batch, seq, width) for g in range(N_PAT)]
    lse = [outs[2 * g + 1].reshape(batch, seq, LANES) for g in range(N_PAT)]
    return o, lse


def _attn_out_kernel(o0_ref, o1_ref, o2_ref, l0_ref, l1_ref, l2_ref, x_ref, mod_ref, w_out_ref,
                     ln_g_ref, ln_b_ref, out_ref):
    lses = [l0_ref[...], l1_ref[...], l2_ref[...]]
    mx = jnp.maximum(jnp.maximum(lses[0], lses[1]), lses[2])
    es = [jnp.exp(l - mx) for l in lses]
    den = es[0] + es[1] + es[2]
    src_lane = lax.broadcasted_iota(jnp.int32, (LANES, D_MODEL), 0)
    head_of_col = lax.broadcasted_iota(jnp.int32, (LANES, D_MODEL), 1) // B_HEAD_DIM
    expand = jnp.logical_and(src_lane < 2 * B_HEADS, src_lane % B_HEADS == head_of_col)
    expand = expand.astype(BF16)
    lane = lax.broadcasted_iota(jnp.int32, lses[0].shape, 1)
    o = o2_ref[...].astype(F32)
    base = o
    for e, o_ref in zip(es[:2], (o0_ref, o1_ref)):
        wts = e / den
        head = wts.astype(BF16).astype(F32)
        rest = pltpu.roll(wts - head, B_HEADS, axis=1)
        split = jnp.where(lane < B_HEADS, head, jnp.where(lane < 2 * B_HEADS, rest, 0.0))
        w = jnp.dot(split.astype(BF16), expand, preferred_element_type=F32)
        o = o + w * (o_ref[...].astype(F32) - base)
    y = jnp.dot(o.astype(BF16), w_out_ref[...], preferred_element_type=F32)
    mod = mod_ref[...]
    out_ref[...] = _post_norm(x_ref[...], y, mod, ln_g_ref[...], ln_b_ref[...])


def _attn_out(o, lse, x, mod, w_out, ln_g, ln_b):
    batch, seq, _ = x.shape
    tm = 512
    row = lambda a: a.reshape(1, -1)
    tile = lambda w: pl.BlockSpec((None, tm, w), lambda b, i: (b, i, 0))
    return pl.pallas_call(
        _attn_out_kernel,
        grid=(batch, seq // tm),
        in_specs=[tile(D_MODEL)] * 3 + [tile(LANES)] * 3 + [
            tile(D_MODEL),
            pl.BlockSpec((None, 3, D_MODEL), lambda b, i: (b, 0, 0)),
            _resident((D_MODEL, D_MODEL)),
            _resident((1, D_MODEL)),
            _resident((1, D_MODEL)),
        ],
        out_specs=tile(D_MODEL),
        out_shape=jax.ShapeDtypeStruct(x.shape, F32),
        compiler_params=pltpu.CompilerParams(
            dimension_semantics=("arbitrary", "arbitrary"),
            vmem_limit_bytes=VMEM_LIMIT_BYTES),
        name="attn_out",
    )(*o, *lse, x, mod, w_out.astype(BF16), row(ln_g), row(ln_b))


def kernel(x, c, ada_w, ada_b, ln_g, ln_b, a_w_in, a_b_in, a_vn_g, a_vn_b, a_w_s, a_b_s, a_w_out,
           b_w_qkv, b_w_out, mlp_w_up, mlp_w_down):
    mods = _ada_mod(c, ada_w, ada_b)
    x = _mixer_a(x, mods[0], a_w_in[0], a_b_in[0], a_vn_g[0], a_vn_b[0], a_w_s[0], a_b_s[0],
                 a_w_out[0], ln_g[0, 0], ln_b[0, 0])
    x = _mlp(x, mods[1], mlp_w_up[0], mlp_w_down[0], ln_g[0, 1], ln_b[0, 1], residue_in=False)
    kv_local, qkv, x_res = _qkv(x, mods[2], b_w_qkv[0])
    o, lse = _attention(kv_local, qkv)
    x_res = _attn_out(o, lse, x_res, mods[2], b_w_out[0], ln_g[1, 0], ln_b[1, 0])
    return _mlp(x_res, mods[3], mlp_w_up[1], mlp_w_down[1], ln_g[1, 1], ln_b[1, 1],
                residue_in=True)
```

```python
import jax
import jax.numpy as jnp
from jax import lax
from jax.experimental import pallas as pl
from jax.experimental.pallas import tpu as pltpu

D_MODEL = 1024
DEPTH = 2
CHUNK = 128
A_GROUPS = 16
A_GROUP_DIM = D_MODEL // A_GROUPS
B_HEADS = 16
B_HEAD_DIM = D_MODEL // B_HEADS
B_PATTERNS = ((128, 1), (512, 4), (2048, 16))
N_PAT = len(B_PATTERNS)
SPAN = 128
D_FF = 4 * D_MODEL
ALPHA = (2 * DEPTH) ** 0.25
LN_EPS = 1e-5
NEG = -1e30

LANES = 128
SUBLANES = 8
VMEM_LIMIT_BYTES = 56 * 1024 * 1024
SINGLE_BUFFER_MIN_ELEMENTS = 1024 * 1024

RES = 16
MID = 4
SUB = RES // MID

F32 = jnp.float32
BF16 = jnp.bfloat16


def _layer_norm(x, g, b):
    mu = jnp.mean(x, axis=-1, keepdims=True)
    xc = x - mu
    var = jnp.mean(xc * xc, axis=-1, keepdims=True)
    return xc * lax.rsqrt(var + LN_EPS) * g + b


def _resident(shape):
    zeros = (0,) * len(shape)
    elements = 1
    for dim in shape:
        elements *= dim
    if elements < SINGLE_BUFFER_MIN_ELEMENTS:
        return pl.BlockSpec(shape, lambda *_: zeros)
    return pl.BlockSpec(shape, lambda *_: zeros, pipeline_mode=pl.Buffered(1))


def _ada_kernel(c_ref, w_ref, b_ref, o_ref):
    s = jax.nn.silu(c_ref[...]).astype(BF16)
    o_ref[0] = jnp.dot(s, w_ref[0].astype(BF16), preferred_element_type=F32) + b_ref[0]


def _ada_mod(c, ada_w, ada_b):
    batch = c.shape[0]
    rows = SUBLANES
    c_pad = jnp.zeros((rows, D_MODEL), F32).at[:batch].set(c)
    n_sub = DEPTH * 2
    w = ada_w.reshape(n_sub, D_MODEL, 3 * D_MODEL)
    b = ada_b.reshape(n_sub, 1, 3 * D_MODEL)
    tn = 1024
    out = pl.pallas_call(
        _ada_kernel,
        grid=(n_sub, 3 * D_MODEL // tn),
        in_specs=[
            pl.BlockSpec((rows, D_MODEL), lambda i, j: (0, 0)),
            pl.BlockSpec((1, D_MODEL, tn), lambda i, j: (i, 0, j)),
            pl.BlockSpec((1, 1, tn), lambda i, j: (i, 0, j)),
        ],
        out_specs=pl.BlockSpec((1, rows, tn), lambda i, j: (i, 0, j)),
        out_shape=jax.ShapeDtypeStruct((n_sub, rows, 3 * D_MODEL), F32),
        name="ada_mod",
    )(c_pad, w, b)
    return out[:, :batch].reshape(n_sub, batch, 3, D_MODEL)


def _modulate(x, mod):
    shift = mod[0:1]
    scale = mod[1:2]
    return (x * (1.0 + scale) + shift).astype(BF16)


def _post_norm(x, y, mod, g, b):
    gate = 1.0 + mod[2:3]
    return _layer_norm(ALPHA * x + gate * y, g, b)


def _mixer_a_kernel(x_ref, mod_ref, w_in_ref, b_in_ref, vn_g_ref, vn_b_ref, ws_ref, bs_ref,
                    w_out_ref, ln_g_ref, ln_b_ref, o_ref, ws_scr, u_scr, v_scr, g_scr):
    tm = x_ref.shape[0]

    @pl.when(jnp.logical_and(pl.program_id(0) == 0, pl.program_id(1) == 0))
    def _():
        t = lax.broadcasted_iota(jnp.int32, (CHUNK, CHUNK), 0)
        s = lax.broadcasted_iota(jnp.int32, (CHUNK, CHUNK), 1)
        causal = s <= t
        for g in range(A_GROUPS):
            ws_scr[g] = jnp.where(causal, ws_ref[g], 0.0).astype(BF16)

    x = x_ref[...]
    mod = mod_ref[...]
    h = _modulate(x, mod)
    v = jnp.dot(h, w_in_ref[:, D_MODEL:], preferred_element_type=F32) + b_in_ref[:, D_MODEL:]
    u = jnp.dot(h, w_in_ref[:, :D_MODEL], preferred_element_type=F32) + b_in_ref[:, :D_MODEL]
    v = _layer_norm(jax.nn.gelu(v), vn_g_ref[...], vn_b_ref[...])
    v_scr[...] = v.astype(BF16)
    u_scr[...] = jax.nn.gelu(u)

    low_half = lax.broadcasted_iota(jnp.int32, (CHUNK, LANES), 1) < A_GROUP_DIM

    def chunk_body(c, carry):
        r0 = pl.multiple_of(c * CHUNK, CHUNK)
        for gp in range(A_GROUPS // 2):
            cols = slice(gp * LANES, (gp + 1) * LANES)
            vb = v_scr[pl.ds(r0, CHUNK), cols]
            za = jnp.dot(ws_scr[2 * gp], vb, preferred_element_type=F32)
            zb = jnp.dot(ws_scr[2 * gp + 1], vb, preferred_element_type=F32)
            z = jnp.where(low_half, za, zb) + bs_ref[:, cols]
            u = u_scr[pl.ds(r0, CHUNK), cols]
            g_scr[pl.ds(r0, CHUNK), cols] = (u * z).astype(BF16)
        return carry

    lax.fori_loop(0, tm // CHUNK, chunk_body, 0, unroll=True)

    y = jnp.dot(g_scr[...], w_out_ref[...], preferred_element_type=F32)
    o_ref[...] = _post_norm(x, y, mod, ln_g_ref[...], ln_b_ref[...])


def _mixer_a(x, mod, w_in, b_in, vn_g, vn_b, w_s, b_s, w_out, ln_g, ln_b):
    batch, seq, _ = x.shape
    tm = 512
    row = lambda a: a.reshape(1, -1)
    bs_plane = jnp.repeat(b_s.T, A_GROUP_DIM, axis=1)
    return pl.pallas_call(
        _mixer_a_kernel,
        grid=(batch, seq // tm),
        in_specs=[
            pl.BlockSpec((None, tm, D_MODEL), lambda b, i: (b, i, 0)),
            pl.BlockSpec((None, 3, D_MODEL), lambda b, i: (b, 0, 0)),
            _resident((D_MODEL, 2 * D_MODEL)),
            _resident((1, 2 * D_MODEL)),
            _resident((1, D_MODEL)),
            _resident((1, D_MODEL)),
            _resident((A_GROUPS, CHUNK, CHUNK)),
            _resident((CHUNK, D_MODEL)),
            _resident((D_MODEL, D_MODEL)),
            _resident((1, D_MODEL)),
            _resident((1, D_MODEL)),
        ],
        out_specs=pl.BlockSpec((None, tm, D_MODEL), lambda b, i: (b, i, 0)),
        out_shape=jax.ShapeDtypeStruct(x.shape, F32),
        scratch_shapes=[
            pltpu.VMEM((A_GROUPS, CHUNK, CHUNK), BF16),
            pltpu.VMEM((tm, D_MODEL), F32),
            pltpu.VMEM((tm, D_MODEL), BF16),
            pltpu.VMEM((tm, D_MODEL), BF16),
        ],
        compiler_params=pltpu.CompilerParams(
            dimension_semantics=("arbitrary", "arbitrary"),
            vmem_limit_bytes=VMEM_LIMIT_BYTES),
        name="mixer_a",
    )(x, mod, w_in.astype(BF16), row(b_in), row(vn_g), row(vn_b), w_s, bs_plane,
      w_out.astype(BF16), row(ln_g), row(ln_b))


MLP_ROW_PARTS = 2


def _mlp_rows(x, mod, w_up_ref, w_down_ref, ln_g_ref, ln_b_ref):
    tm = x.shape[0]
    h = _modulate(x, mod)
    a = jnp.dot(h, w_up_ref[...], preferred_element_type=F32)
    r = jnp.square(jnp.maximum(a, 0.0)).astype(BF16)
    part = tm // MLP_ROW_PARTS
    outs = []
    for p in range(MLP_ROW_PARTS):
        rows = slice(p * part, (p + 1) * part)
        y = jnp.dot(r[rows], w_down_ref[...], preferred_element_type=F32)
        outs.append(_post_norm(x[rows], y, mod, ln_g_ref[...], ln_b_ref[...]))
    return jnp.concatenate(outs, axis=0)


def _mlp_kernel(x_ref, mod_ref, w_up_ref, w_down_ref, ln_g_ref, ln_b_ref, o_ref):
    o_ref[...] = _mlp_rows(x_ref[...], mod_ref[...], w_up_ref, w_down_ref, ln_g_ref, ln_b_ref)


def _mlp(x, mod, w_up, w_down, ln_g, ln_b):
    batch, seq, _ = x.shape
    tm = 512
    row = lambda a: a.reshape(1, -1)
    nat_spec = pl.BlockSpec((None, tm, D_MODEL), lambda b, i: (b, i, 0))
    nat_shape = jax.ShapeDtypeStruct((batch, seq, D_MODEL), F32)
    return pl.pallas_call(
        _mlp_kernel,
        grid=(batch, seq // tm),
        in_specs=[
            nat_spec,
            pl.BlockSpec((None, 3, D_MODEL), lambda b, i: (b, 0, 0)),
            _resident((D_MODEL, D_FF)),
            _resident((D_FF, D_MODEL)),
            _resident((1, D_MODEL)),
            _resident((1, D_MODEL)),
        ],
        out_specs=nat_spec,
        out_shape=nat_shape,
        compiler_params=pltpu.CompilerParams(
            dimension_semantics=("arbitrary", "arbitrary"),
            vmem_limit_bytes=VMEM_LIMIT_BYTES),
        name="mlp",
    )(x, mod, w_up.astype(BF16), w_down.astype(BF16), row(ln_g), row(ln_b))


LOCAL_KV_BLOCKS = (1, 2)


def _qkv_kernel(x_ref, mod_ref, w_ref, kv_ref, qkv_ref, x_res_ref):
    tm = x_ref.shape[0]
    width = B_HEADS * B_HEAD_DIM
    mod = mod_ref[...]
    x = x_ref[...]
    x_res = jnp.swapaxes(x.reshape(tm // RES, RES, D_MODEL), 0, 1)
    x_res_ref[...] = x_res
    h_nat = _modulate(x, mod)
    for j, c in enumerate(LOCAL_KV_BLOCKS):
        kv_ref[:, j * width:(j + 1) * width] = jnp.dot(
            h_nat, w_ref[:, c * width:(c + 1) * width], preferred_element_type=F32).astype(BF16)
    h_res = _modulate(x_res.reshape(tm, D_MODEL), mod)
    others = [c for c in range(w_ref.shape[1] // width) if c not in LOCAL_KV_BLOCKS]
    for j, c in enumerate(others):
        out = jnp.dot(h_res, w_ref[:, c * width:(c + 1) * width], preferred_element_type=F32)
        qkv_ref[:, :, j * width:(j + 1) * width] = out.astype(BF16).reshape(RES, tm // RES, width)


def _qkv(x, mod, w_qkv):
    batch, seq, _ = x.shape
    n_out = w_qkv.shape[1]
    width = B_HEADS * B_HEAD_DIM
    n_local = len(LOCAL_KV_BLOCKS) * width
    tm = 512
    res_block = lambda w: pl.BlockSpec((None, RES, tm // RES, w), lambda b, i: (b, 0, i, 0))
    res_shape = lambda w, dtype: jax.ShapeDtypeStruct((batch, RES, seq // RES, w), dtype)
    kv, qkv, x_res = pl.pallas_call(
        _qkv_kernel,
        grid=(batch, seq // tm),
        in_specs=[
            pl.BlockSpec((None, tm, D_MODEL), lambda b, i: (b, i, 0)),
            pl.BlockSpec((None, 3, D_MODEL), lambda b, i: (b, 0, 0)),
            _resident((D_MODEL, n_out)),
        ],
        out_specs=[pl.BlockSpec((None, tm, n_local), lambda b, i: (b, i, 0)),
                   res_block(n_out - n_local), res_block(D_MODEL)],
        out_shape=[jax.ShapeDtypeStruct((batch, seq, n_local), BF16),
                   res_shape(n_out - n_local, BF16), res_shape(D_MODEL, F32)],
        compiler_params=pltpu.CompilerParams(
            dimension_semantics=("arbitrary", "arbitrary"),
            vmem_limit_bytes=VMEM_LIMIT_BYTES),
        name="qkv_proj",
    )(x, mod, w_qkv.astype(BF16))
    return kv, qkv.reshape(batch, seq, n_out - n_local), x_res.reshape(batch, seq, D_MODEL)


HALF = SPAN // RES
QUARTER = SPAN // SUB
UNITS = 2 * N_PAT


def _build_bias_tables(bias_scr):
    row = lax.broadcasted_iota(jnp.int32, (SPAN, 2 * SPAN), 0)
    col = lax.broadcasted_iota(jnp.int32, (SPAN, 2 * SPAN), 1)
    diff_local = SPAN + RES * (row % HALF) + row // HALF - col
    diff_mid = (SUB * (row % QUARTER) + row // QUARTER
                - SUB * (col % (2 * QUARTER) - QUARTER) - col // (2 * QUARTER))
    diff_wide = SPAN + row - col
    has_prev = (None, None, None, col >= SPAN, col % (2 * QUARTER) >= QUARTER)
    for t, diff in enumerate((diff_local, diff_mid, diff_wide, diff_local, diff_mid)):
        dilation = B_PATTERNS[t % N_PAT][1]
        valid = jnp.logical_and(diff >= 0, diff <= SPAN)
        if has_prev[t] is not None:
            valid = jnp.logical_and(valid, has_prev[t])
        dist = (dilation * diff).astype(F32)
        for hd in range(B_HEADS):
            slope = 2.0 ** (-8.0 * (hd + 1) / B_HEADS)
            bias_scr[t, hd] = jnp.where(valid, -slope * dist, NEG)


def _attn_kernel(q1_ref, k1p_ref, k1c_ref, v1p_ref, v1c_ref,
                 q2_ref, k2p_ref, k2c_ref, v2p_ref, v2c_ref,
                 q3_ref, k3_ref, v3_ref,
                 o1_ref, l1_ref, o2_ref, l2_ref, o3_ref, l3_ref, bias_scr, m_scr, l_scr):
    step = pl.program_id(1)

    @pl.when(jnp.logical_and(pl.program_id(0) == 0, step == 0))
    def _():
        _build_bias_tables(bias_scr)
        m_scr[...] = jnp.zeros(m_scr.shape, F32)
        l_scr[...] = jnp.ones(l_scr.shape, F32)

    lane = lax.broadcasted_iota(jnp.int32, (SPAN, LANES), 1)
    low_half = lane < B_HEAD_DIM
    scale = B_HEAD_DIM ** -0.5
    q_mask = (jnp.where(low_half, scale, 0.0).astype(BF16),
              jnp.where(low_half, 0.0, scale).astype(BF16))

    mid_steps = pl.num_programs(1) // MID
    table_local_a = jnp.where(step == 0, N_PAT, 0)
    table_mid_a = jnp.where(step % mid_steps == 0, N_PAT + 1, 1)

    def unit(q2, k2, v2, table, hp, own_block_only=False):
        q_both = jnp.concatenate([q2 * q_mask[0], q2 * q_mask[1]], axis=0)
        s_both = lax.dot_general(q_both, k2, (((1,), (1,)), ((), ())),
                                 preferred_element_type=F32)
        ps, ms, ls = [], [], []
        for j in range(2):
            hd = 2 * hp + j
            s = s_both[j * SPAN:(j + 1) * SPAN]
            if own_block_only:
                s = s + bias_scr[table, hd, :, SPAN:]
            else:
                s = s + bias_scr[table, hd]
            m = jnp.max(s, axis=-1, keepdims=True)
            p = jnp.exp(s - m)
            ls.append(jnp.sum(p, axis=-1, keepdims=True))
            ms.append(m)
            ps.append(p.astype(BF16))
        pv = jnp.dot(jnp.concatenate(ps, axis=0), v2, preferred_element_type=F32)
        o = jnp.where(low_half, pv[:SPAN] / ls[0], pv[SPAN:] / ls[1])
        return o, (ms, ls)

    def put(u, hp, stats):
        for j in range(2):
            hd = 2 * hp + j
            m_scr[u, :, hd:hd + 1] = stats[0][j]
            l_scr[u, :, hd:hd + 1] = stats[1][j]

    for hp in range(B_HEADS // 2):
        cols = slice(hp * LANES, (hp + 1) * LANES)

        q1 = q1_ref[:, :, cols].astype(F32).reshape(RES, 2, HALF, LANES)
        q1a = q1[:, 0].reshape(SPAN, LANES).astype(BF16)
        q1b = q1[:, 1].reshape(SPAN, LANES).astype(BF16)
        k1c = k1c_ref[:, cols]
        v1c = v1c_ref[:, cols]
        oa, sa = unit(q1a, jnp.concatenate([k1p_ref[:, cols], k1c[:SPAN]], axis=0),
                      jnp.concatenate([v1p_ref[:, cols], v1c[:SPAN]], axis=0),
                      table_local_a, hp)
        ob, sb = unit(q1b, k1c, v1c, 0, hp)
        o1 = jnp.concatenate([oa.reshape(RES, HALF, LANES), ob.reshape(RES, HALF, LANES)], axis=1)
        o1_ref[:, :, cols] = o1.astype(BF16)
        put(0, hp, sa)
        put(1, hp, sb)

        q2 = q2_ref[:, :, cols]
        k2p, k2c = k2p_ref[:, :, cols], k2c_ref[:, :, cols]
        v2p, v2c = v2p_ref[:, :, cols], v2c_ref[:, :, cols]
        q2a = q2[:, :QUARTER].reshape(SPAN, LANES)
        q2b = q2[:, QUARTER:].reshape(SPAN, LANES)
        k2a = jnp.concatenate([k2p, k2c[:, :QUARTER]], axis=1).reshape(2 * SPAN, LANES)
        v2a = jnp.concatenate([v2p, v2c[:, :QUARTER]], axis=1).reshape(2 * SPAN, LANES)
        oa, sa = unit(q2a, k2a, v2a, table_mid_a, hp)
        ob, sb = unit(q2b, k2c.reshape(2 * SPAN, LANES), v2c.reshape(2 * SPAN, LANES), 1, hp)
        o2_ref[:, :QUARTER, cols] = oa.reshape(SUB, QUARTER, LANES).astype(BF16)
        o2_ref[:, QUARTER:, cols] = ob.reshape(SUB, QUARTER, LANES).astype(BF16)
        put(2, hp, sa)
        put(3, hp, sb)

        q3, k3, v3 = q3_ref[:, cols], k3_ref[:, cols], v3_ref[:, cols]
        oa, sa = unit(q3[:SPAN], k3[:SPAN], v3[:SPAN], 2, hp, own_block_only=True)
        ob, sb = unit(q3[SPAN:], k3, v3, 2, hp)
        o3_ref[:SPAN, cols] = oa.astype(BF16)
        o3_ref[SPAN:, cols] = ob.astype(BF16)
        put(4, hp, sa)
        put(5, hp, sb)

    l1a, l1b, l2a, l2b, l3a, l3b = [m_scr[u] + jnp.log(l_scr[u]) for u in range(UNITS)]
    l1_ref[...] = jnp.concatenate(
        [l1a.reshape(RES, HALF, LANES), l1b.reshape(RES, HALF, LANES)], axis=1)
    l2_ref[:, :QUARTER] = l2a.reshape(SUB, QUARTER, LANES)
    l2_ref[:, QUARTER:] = l2b.reshape(SUB, QUARTER, LANES)
    l3_ref[:SPAN] = l3a
    l3_ref[SPAN:] = l3b


def _attention(kv_local, qkv):
    batch, seq, n_qkv = qkv.shape
    width = B_HEADS * B_HEAD_DIM
    per = seq // RES
    n_steps = seq // (2 * SPAN)
    by_res = qkv.reshape(batch, RES, per, n_qkv)
    by_sub = qkv.reshape(batch, SUB, MID, per, n_qkv)

    def col(g, part):
        return 0 if g == 0 else 3 * g - 2 + part

    mid_steps = n_steps // MID

    def mid_spec(rows, part, prev):
        def index_map(b, s):
            blk = s % mid_steps
            if prev:
                blk = jnp.maximum(2 * blk - 1, 0)
            return (b, 0, s // mid_steps, blk, col(1, part))
        return pl.BlockSpec((None, SUB, None, rows, width), index_map)

    in_specs = [
        pl.BlockSpec((None, RES, 2 * HALF, width), lambda b, s: (b, 0, s, col(0, 0))),
        pl.BlockSpec((None, SPAN, width), lambda b, s: (b, jnp.maximum(2 * s - 1, 0), 0)),
        pl.BlockSpec((None, 2 * SPAN, width), lambda b, s: (b, s, 0)),
        pl.BlockSpec((None, SPAN, width), lambda b, s: (b, jnp.maximum(2 * s - 1, 0), 1)),
        pl.BlockSpec((None, 2 * SPAN, width), lambda b, s: (b, s, 1)),
        mid_spec(2 * QUARTER, 0, False),
        mid_spec(QUARTER, 1, True), mid_spec(2 * QUARTER, 1, False),
        mid_spec(QUARTER, 2, True), mid_spec(2 * QUARTER, 2, False),
        pl.BlockSpec((None, None, per, width), lambda b, s: (b, s, 0, col(2, 0))),
        pl.BlockSpec((None, None, per, width), lambda b, s: (b, s, 0, col(2, 1))),
        pl.BlockSpec((None, None, per, width), lambda b, s: (b, s, 0, col(2, 2))),
    ]
    operands = [by_res] + [kv_local] * 4 + [by_sub] * 5 + [by_res] * 3

    out_specs, out_shapes = [], []

    def outs_for(w, dtype):
        return (
            [pl.BlockSpec((None, RES, 2 * HALF, w), lambda b, s: (b, 0, s, 0)),
             pl.BlockSpec((None, SUB, None, 2 * QUARTER, w),
                          lambda b, s: (b, 0, s // mid_steps, s % mid_steps, 0)),
             pl.BlockSpec((None, None, per, w), lambda b, s: (b, s, 0, 0))],
            [jax.ShapeDtypeStruct((batch, RES, per, w), dtype),
             jax.ShapeDtypeStruct((batch, SUB, MID, per, w), dtype),
             jax.ShapeDtypeStruct((batch, RES, per, w), dtype)])
    o_specs, o_shapes = outs_for(width, BF16)
    l_specs, l_shapes = outs_for(LANES, F32)
    for g in range(N_PAT):
        out_specs += [o_specs[g], l_specs[g]]
        out_shapes += [o_shapes[g], l_shapes[g]]

    outs = pl.pallas_call(
        _attn_kernel,
        grid=(batch, n_steps),
        in_specs=in_specs,
        out_specs=out_specs,
        out_shape=out_shapes,
        scratch_shapes=[pltpu.VMEM((N_PAT + 2, B_HEADS, SPAN, 2 * SPAN), F32),
                        pltpu.VMEM((UNITS, SPAN, LANES), F32),
                        pltpu.VMEM((UNITS, SPAN, LANES), F32)],
        compiler_params=pltpu.CompilerParams(
            dimension_semantics=("arbitrary", "arbitrary"),
            vmem_limit_bytes=VMEM_LIMIT_BYTES),
        name="dilated_attn",
    )(*operands)
    o = [outs[2 * g].reshape(batch, seq, width) for g in range(N_PAT)]
    lse = [outs[2 * g + 1].reshape(batch, seq, LANES) for g in range(N_PAT)]
    return o, lse


def _attn_out_kernel(o0_ref, o1_ref, o2_ref, l0_ref, l1_ref, l2_ref, x_ref, mod_ref, w_out_ref,
                     ln_g_ref, ln_b_ref, mlp_mod_ref, w_up_ref, w_down_ref, mlp_g_ref, mlp_b_ref,
                     out_ref):
    tm = out_ref.shape[0]
    read = lambda ref: ref[...].reshape(tm, ref.shape[-1])
    lses = [read(l0_ref), read(l1_ref), read(l2_ref)]
    mx = jnp.maximum(jnp.maximum(lses[0], lses[1]), lses[2])
    es = [jnp.exp(l - mx) for l in lses]
    den = es[0] + es[1] + es[2]
    src_lane = lax.broadcasted_iota(jnp.int32, (LANES, D_MODEL), 0)
    head_of_col = lax.broadcasted_iota(jnp.int32, (LANES, D_MODEL), 1) // B_HEAD_DIM
    expand = jnp.logical_and(src_lane < 2 * B_HEADS, src_lane % B_HEADS == head_of_col)
    expand = expand.astype(BF16)
    lane = lax.broadcasted_iota(jnp.int32, lses[0].shape, 1)
    o = read(o2_ref).astype(F32)
    base = o
    for e, o_ref in zip(es[:2], (o0_ref, o1_ref)):
        wts = e / den
        head = wts.astype(BF16).astype(F32)
        rest = pltpu.roll(wts - head, B_HEADS, axis=1)
        split = jnp.where(lane < B_HEADS, head, jnp.where(lane < 2 * B_HEADS, rest, 0.0))
        w = jnp.dot(split.astype(BF16), expand, preferred_element_type=F32)
        o = o + w * (read(o_ref).astype(F32) - base)
    y = jnp.dot(o.astype(BF16), w_out_ref[...], preferred_element_type=F32)
    x = _post_norm(read(x_ref), y, mod_ref[...], ln_g_ref[...], ln_b_ref[...])
    out = _mlp_rows(x, mlp_mod_ref[...], w_up_ref, w_down_ref, mlp_g_ref, mlp_b_ref)
    out_ref[...] = jnp.swapaxes(out.reshape(RES, tm // RES, D_MODEL), 0, 1).reshape(tm, D_MODEL)


def _attn_out_mlp(o, lse, x, mod, w_out, ln_g, ln_b, mlp_mod, w_up, w_down, mlp_g, mlp_b):
    batch, seq, _ = x.shape
    tm = 512
    row = lambda a: a.reshape(1, -1)
    by_res = lambda a: a.reshape(batch, RES, seq // RES, a.shape[-1])
    tile = lambda w: pl.BlockSpec((None, RES, tm // RES, w), lambda b, i: (b, 0, i, 0))
    per_batch = pl.BlockSpec((None, 3, D_MODEL), lambda b, i: (b, 0, 0))
    return pl.pallas_call(
        _attn_out_kernel,
        grid=(batch, seq // tm),
        in_specs=[tile(D_MODEL)] * 3 + [tile(LANES)] * 3 + [
            tile(D_MODEL),
            per_batch,
            _resident((D_MODEL, D_MODEL)),
            _resident((1, D_MODEL)),
            _resident((1, D_MODEL)),
            per_batch,
            _resident((D_MODEL, D_FF)),
            _resident((D_FF, D_MODEL)),
            _resident((1, D_MODEL)),
            _resident((1, D_MODEL)),
        ],
        out_specs=pl.BlockSpec((None, tm, D_MODEL), lambda b, i: (b, i, 0)),
        out_shape=jax.ShapeDtypeStruct(x.shape, F32),
        compiler_params=pltpu.CompilerParams(
            dimension_semantics=("arbitrary", "arbitrary"),
            vmem_limit_bytes=VMEM_LIMIT_BYTES),
        name="attn_out_mlp",
    )(*[by_res(a) for a in o], *[by_res(a) for a in lse], by_res(x), mod, w_out.astype(BF16),
      row(ln_g), row(ln_b), mlp_mod, w_up.astype(BF16), w_down.astype(BF16), row(mlp_g),
      row(mlp_b))


def kernel(x, c, ada_w, ada_b, ln_g, ln_b, a_w_in, a_b_in, a_vn_g, a_vn_b, a_w_s, a_b_s, a_w_out,
           b_w_qkv, b_w_out, mlp_w_up, mlp_w_down):
    mods = _ada_mod(c, ada_w, ada_b)
    x = _mixer_a(x, mods[0], a_w_in[0], a_b_in[0], a_vn_g[0], a_vn_b[0], a_w_s[0], a_b_s[0],
                 a_w_out[0], ln_g[0, 0], ln_b[0, 0])
    x = _mlp(x, mods[1], mlp_w_up[0], mlp_w_down[0], ln_g[0, 1], ln_b[0, 1])
    kv_local, qkv, x_res = _qkv(x, mods[2], b_w_qkv[0])
    o, lse = _attention(kv_local, qkv)
    return _attn_out_mlp(o, lse, x_res, mods[2], b_w_out[0], ln_g[1, 0], ln_b[1, 0],
                         mods[3], mlp_w_up[1], mlp_w_down[1], ln_g[1, 1], ln_b[1, 1])
```

```python
import jax
import jax.numpy as jnp
from jax import lax
from jax.experimental import pallas as pl
from jax.experimental.pallas import tpu as pltpu

D_MODEL = 1024
DEPTH = 2
CHUNK = 128
A_GROUPS = 16
A_GROUP_DIM = D_MODEL // A_GROUPS
B_HEADS = 16
B_HEAD_DIM = D_MODEL // B_HEADS
B_PATTERNS = ((128, 1), (512, 4), (2048, 16))
N_PAT = len(B_PATTERNS)
SPAN = 128
D_FF = 4 * D_MODEL
ALPHA = (2 * DEPTH) ** 0.25
LN_EPS = 1e-5
NEG = -1e30

LANES = 128
SUBLANES = 8
VMEM_LIMIT_BYTES = 56 * 1024 * 1024
SINGLE_BUFFER_MIN_ELEMENTS = 1024 * 1024

RES = 16
MID = 4
SUB = RES // MID

F32 = jnp.float32
BF16 = jnp.bfloat16


def _layer_norm(x, g, b):
    mu = jnp.mean(x, axis=-1, keepdims=True)
    xc = x - mu
    var = jnp.mean(xc * xc, axis=-1, keepdims=True)
    return xc * lax.rsqrt(var + LN_EPS) * g + b


def _resident(shape):
    zeros = (0,) * len(shape)
    elements = 1
    for dim in shape:
        elements *= dim
    if elements < SINGLE_BUFFER_MIN_ELEMENTS:
        return pl.BlockSpec(shape, lambda *_: zeros)
    return pl.BlockSpec(shape, lambda *_: zeros, pipeline_mode=pl.Buffered(1))


def _ada_kernel(c_ref, w_ref, b_ref, o_ref):
    s = jax.nn.silu(c_ref[...]).astype(BF16)
    o_ref[0] = jnp.dot(s, w_ref[0].astype(BF16), preferred_element_type=F32) + b_ref[0]


def _ada_mod(c, ada_w, ada_b):
    batch = c.shape[0]
    rows = SUBLANES
    c_pad = jnp.zeros((rows, D_MODEL), F32).at[:batch].set(c)
    n_sub = DEPTH * 2
    w = ada_w.reshape(n_sub, D_MODEL, 3 * D_MODEL)
    b = ada_b.reshape(n_sub, 1, 3 * D_MODEL)
    tn = 1024
    out = pl.pallas_call(
        _ada_kernel,
        grid=(n_sub, 3 * D_MODEL // tn),
        in_specs=[
            pl.BlockSpec((rows, D_MODEL), lambda i, j: (0, 0)),
            pl.BlockSpec((1, D_MODEL, tn), lambda i, j: (i, 0, j)),
            pl.BlockSpec((1, 1, tn), lambda i, j: (i, 0, j)),
        ],
        out_specs=pl.BlockSpec((1, rows, tn), lambda i, j: (i, 0, j)),
        out_shape=jax.ShapeDtypeStruct((n_sub, rows, 3 * D_MODEL), F32),
        name="ada_mod",
    )(c_pad, w, b)
    return out[:, :batch].reshape(n_sub, batch, 3, D_MODEL)


def _modulate(x, mod):
    shift = mod[0:1]
    scale = mod[1:2]
    return (x * (1.0 + scale) + shift).astype(BF16)


def _post_norm(x, y, mod, g, b):
    gate = 1.0 + mod[2:3]
    return _layer_norm(ALPHA * x + gate * y, g, b)


def _mixer_a_kernel(x_ref, mod_ref, w_in_ref, b_in_ref, vn_g_ref, vn_b_ref, ws_ref, bs_ref,
                    w_out_ref, ln_g_ref, ln_b_ref, mlp_mod_ref, w_up_ref, w_down_ref, mlp_g_ref,
                    mlp_b_ref, o_ref, ws_scr, u_scr, v_scr, g_scr):
    tm = x_ref.shape[0]

    @pl.when(jnp.logical_and(pl.program_id(0) == 0, pl.program_id(1) == 0))
    def _():
        t = lax.broadcasted_iota(jnp.int32, (CHUNK, CHUNK), 0)
        s = lax.broadcasted_iota(jnp.int32, (CHUNK, CHUNK), 1)
        causal = s <= t
        for g in range(A_GROUPS):
            ws_scr[g] = jnp.where(causal, ws_ref[g], 0.0).astype(BF16)

    x = x_ref[...]
    mod = mod_ref[...]
    h = _modulate(x, mod)
    v = jnp.dot(h, w_in_ref[:, D_MODEL:], preferred_element_type=F32) + b_in_ref[:, D_MODEL:]
    u = jnp.dot(h, w_in_ref[:, :D_MODEL], preferred_element_type=F32) + b_in_ref[:, :D_MODEL]
    v = _layer_norm(jax.nn.gelu(v), vn_g_ref[...], vn_b_ref[...])
    v_scr[...] = v.astype(BF16)
    u_scr[...] = jax.nn.gelu(u)

    low_half = lax.broadcasted_iota(jnp.int32, (CHUNK, LANES), 1) < A_GROUP_DIM

    def chunk_body(c, carry):
        r0 = pl.multiple_of(c * CHUNK, CHUNK)
        for gp in range(A_GROUPS // 2):
            cols = slice(gp * LANES, (gp + 1) * LANES)
            vb = v_scr[pl.ds(r0, CHUNK), cols]
            za = jnp.dot(ws_scr[2 * gp], vb, preferred_element_type=F32)
            zb = jnp.dot(ws_scr[2 * gp + 1], vb, preferred_element_type=F32)
            z = jnp.where(low_half, za, zb) + bs_ref[:, cols]
            u = u_scr[pl.ds(r0, CHUNK), cols]
            g_scr[pl.ds(r0, CHUNK), cols] = (u * z).astype(BF16)
        return carry

    lax.fori_loop(0, tm // CHUNK, chunk_body, 0, unroll=True)

    y = jnp.dot(g_scr[...], w_out_ref[...], preferred_element_type=F32)
    x = _post_norm(x, y, mod, ln_g_ref[...], ln_b_ref[...])
    o_ref[...] = _mlp_rows(x, mlp_mod_ref[...], w_up_ref, w_down_ref, mlp_g_ref, mlp_b_ref)


def _mixer_a_mlp(x, mod, w_in, b_in, vn_g, vn_b, w_s, b_s, w_out, ln_g, ln_b,
                 mlp_mod, w_up, w_down, mlp_g, mlp_b):
    batch, seq, _ = x.shape
    tm = 512
    row = lambda a: a.reshape(1, -1)
    bs_plane = jnp.repeat(b_s.T, A_GROUP_DIM, axis=1)
    return pl.pallas_call(
        _mixer_a_kernel,
        grid=(batch, seq // tm),
        in_specs=[
            pl.BlockSpec((None, tm, D_MODEL), lambda b, i: (b, i, 0)),
            pl.BlockSpec((None, 3, D_MODEL), lambda b, i: (b, 0, 0)),
            _resident((D_MODEL, 2 * D_MODEL)),
            _resident((1, 2 * D_MODEL)),
            _resident((1, D_MODEL)),
            _resident((1, D_MODEL)),
            _resident((A_GROUPS, CHUNK, CHUNK)),
            _resident((CHUNK, D_MODEL)),
            _resident((D_MODEL, D_MODEL)),
            _resident((1, D_MODEL)),
            _resident((1, D_MODEL)),
            pl.BlockSpec((None, 3, D_MODEL), lambda b, i: (b, 0, 0)),
            _resident((D_MODEL, D_FF)),
            _resident((D_FF, D_MODEL)),
            _resident((1, D_MODEL)),
            _resident((1, D_MODEL)),
        ],
        out_specs=pl.BlockSpec((None, tm, D_MODEL), lambda b, i: (b, i, 0)),
        out_shape=jax.ShapeDtypeStruct(x.shape, F32),
        scratch_shapes=[
            pltpu.VMEM((A_GROUPS, CHUNK, CHUNK), BF16),
            pltpu.VMEM((tm, D_MODEL), F32),
            pltpu.VMEM((tm, D_MODEL), BF16),
            pltpu.VMEM((tm, D_MODEL), BF16),
        ],
        compiler_params=pltpu.CompilerParams(
            dimension_semantics=("arbitrary", "arbitrary"),
            vmem_limit_bytes=VMEM_LIMIT_BYTES),
        name="mixer_a_mlp",
    )(x, mod, w_in.astype(BF16), row(b_in), row(vn_g), row(vn_b), w_s, bs_plane,
      w_out.astype(BF16), row(ln_g), row(ln_b), mlp_mod, w_up.astype(BF16),
      w_down.astype(BF16), row(mlp_g), row(mlp_b))


MLP_ROW_PARTS = 2


def _mlp_rows(x, mod, w_up_ref, w_down_ref, ln_g_ref, ln_b_ref):
    tm = x.shape[0]
    h = _modulate(x, mod)
    a = jnp.dot(h, w_up_ref[...], preferred_element_type=F32)
    r = jnp.square(jnp.maximum(a, 0.0)).astype(BF16)
    part = tm // MLP_ROW_PARTS
    outs = []
    for p in range(MLP_ROW_PARTS):
        rows = slice(p * part, (p + 1) * part)
        y = jnp.dot(r[rows], w_down_ref[...], preferred_element_type=F32)
        outs.append(_post_norm(x[rows], y, mod, ln_g_ref[...], ln_b_ref[...]))
    return jnp.concatenate(outs, axis=0)


LOCAL_KV_BLOCKS = (1, 2)


def _qkv_kernel(x_ref, mod_ref, w_ref, kv_ref, qkv_ref, x_res_ref):
    tm = x_ref.shape[0]
    width = B_HEADS * B_HEAD_DIM
    mod = mod_ref[...]
    x = x_ref[...]
    x_res = jnp.swapaxes(x.reshape(tm // RES, RES, D_MODEL), 0, 1)
    x_res_ref[...] = x_res
    h_nat = _modulate(x, mod)
    for j, c in enumerate(LOCAL_KV_BLOCKS):
        kv_ref[:, j * width:(j + 1) * width] = jnp.dot(
            h_nat, w_ref[:, c * width:(c + 1) * width], preferred_element_type=F32).astype(BF16)
    h_res = _modulate(x_res.reshape(tm, D_MODEL), mod)
    others = [c for c in range(w_ref.shape[1] // width) if c not in LOCAL_KV_BLOCKS]
    for j, c in enumerate(others):
        out = jnp.dot(h_res, w_ref[:, c * width:(c + 1) * width], preferred_element_type=F32)
        qkv_ref[:, :, j * width:(j + 1) * width] = out.astype(BF16).reshape(RES, tm // RES, width)


def _qkv(x, mod, w_qkv):
    batch, seq, _ = x.shape
    n_out = w_qkv.shape[1]
    width = B_HEADS * B_HEAD_DIM
    n_local = len(LOCAL_KV_BLOCKS) * width
    tm = 512
    res_block = lambda w: pl.BlockSpec((None, RES, tm // RES, w), lambda b, i: (b, 0, i, 0))
    res_shape = lambda w, dtype: jax.ShapeDtypeStruct((batch, RES, seq // RES, w), dtype)
    kv, qkv, x_res = pl.pallas_call(
        _qkv_kernel,
        grid=(batch, seq // tm),
        in_specs=[
            pl.BlockSpec((None, tm, D_MODEL), lambda b, i: (b, i, 0)),
            pl.BlockSpec((None, 3, D_MODEL), lambda b, i: (b, 0, 0)),
            _resident((D_MODEL, n_out)),
        ],
        out_specs=[pl.BlockSpec((None, tm, n_local), lambda b, i: (b, i, 0)),
                   res_block(n_out - n_local), res_block(D_MODEL)],
        out_shape=[jax.ShapeDtypeStruct((batch, seq, n_local), BF16),
                   res_shape(n_out - n_local, BF16), res_shape(D_MODEL, F32)],
        compiler_params=pltpu.CompilerParams(
            dimension_semantics=("arbitrary", "arbitrary"),
            vmem_limit_bytes=VMEM_LIMIT_BYTES),
        name="qkv_proj",
    )(x, mod, w_qkv.astype(BF16))
    return kv, qkv.reshape(batch, seq, n_out - n_local), x_res.reshape(batch, seq, D_MODEL)


HALF = SPAN // RES
QUARTER = SPAN // SUB
UNITS = 2 * N_PAT


def _build_bias_tables(bias_scr):
    row = lax.broadcasted_iota(jnp.int32, (SPAN, 2 * SPAN), 0)
    col = lax.broadcasted_iota(jnp.int32, (SPAN, 2 * SPAN), 1)
    diff_local = SPAN + RES * (row % HALF) + row // HALF - col
    diff_mid = (SUB * (row % QUARTER) + row // QUARTER
                - SUB * (col % (2 * QUARTER) - QUARTER) - col // (2 * QUARTER))
    diff_wide = SPAN + row - col
    has_prev = (None, None, None, col >= SPAN, col % (2 * QUARTER) >= QUARTER)
    for t, diff in enumerate((diff_local, diff_mid, diff_wide, diff_local, diff_mid)):
        dilation = B_PATTERNS[t % N_PAT][1]
        valid = jnp.logical_and(diff >= 0, diff <= SPAN)
        if has_prev[t] is not None:
            valid = jnp.logical_and(valid, has_prev[t])
        dist = (dilation * diff).astype(F32)
        for hd in range(B_HEADS):
            slope = 2.0 ** (-8.0 * (hd + 1) / B_HEADS)
            bias_scr[t, hd] = jnp.where(valid, -slope * dist, NEG)


def _attn_kernel(q1_ref, k1p_ref, k1c_ref, v1p_ref, v1c_ref,
                 q2_ref, k2p_ref, k2c_ref, v2p_ref, v2c_ref,
                 q3_ref, k3_ref, v3_ref,
                 o1_ref, l1_ref, o2_ref, l2_ref, o3_ref, l3_ref, bias_scr, m_scr, l_scr):
    step = pl.program_id(1)

    @pl.when(jnp.logical_and(pl.program_id(0) == 0, step == 0))
    def _():
        _build_bias_tables(bias_scr)
        m_scr[...] = jnp.zeros(m_scr.shape, F32)
        l_scr[...] = jnp.ones(l_scr.shape, F32)

    lane = lax.broadcasted_iota(jnp.int32, (SPAN, LANES), 1)
    low_half = lane < B_HEAD_DIM
    scale = B_HEAD_DIM ** -0.5
    q_mask = (jnp.where(low_half, scale, 0.0).astype(BF16),
              jnp.where(low_half, 0.0, scale).astype(BF16))

    mid_steps = pl.num_programs(1) // MID
    table_local_a = jnp.where(step == 0, N_PAT, 0)
    table_mid_a = jnp.where(step % mid_steps == 0, N_PAT + 1, 1)

    def unit(q2, k2, v2, table, hp, own_block_only=False):
        q_both = jnp.concatenate([q2 * q_mask[0], q2 * q_mask[1]], axis=0)
        s_both = lax.dot_general(q_both, k2, (((1,), (1,)), ((), ())),
                                 preferred_element_type=F32)
        ps, ms, ls = [], [], []
        for j in range(2):
            hd = 2 * hp + j
            s = s_both[j * SPAN:(j + 1) * SPAN]
            if own_block_only:
                s = s + bias_scr[table, hd, :, SPAN:]
            else:
                s = s + bias_scr[table, hd]
            m = jnp.max(s, axis=-1, keepdims=True)
            p = jnp.exp(s - m)
            ls.append(jnp.sum(p, axis=-1, keepdims=True))
            ms.append(m)
            ps.append(p.astype(BF16))
        pv = jnp.dot(jnp.concatenate(ps, axis=0), v2, preferred_element_type=F32)
        o = jnp.where(low_half, pv[:SPAN] / ls[0], pv[SPAN:] / ls[1])
        return o, (ms, ls)

    def put(u, hp, stats):
        for j in range(2):
            hd = 2 * hp + j
            m_scr[u, :, hd:hd + 1] = stats[0][j]
            l_scr[u, :, hd:hd + 1] = stats[1][j]

    for hp in range(B_HEADS // 2):
        cols = slice(hp * LANES, (hp + 1) * LANES)

        q1 = q1_ref[:, :, cols].astype(F32).reshape(RES, 2, HALF, LANES)
        q1a = q1[:, 0].reshape(SPAN, LANES).astype(BF16)
        q1b = q1[:, 1].reshape(SPAN, LANES).astype(BF16)
        k1c = k1c_ref[:, cols]
        v1c = v1c_ref[:, cols]
        oa, sa = unit(q1a, jnp.concatenate([k1p_ref[:, cols], k1c[:SPAN]], axis=0),
                      jnp.concatenate([v1p_ref[:, cols], v1c[:SPAN]], axis=0),
                      table_local_a, hp)
        ob, sb = unit(q1b, k1c, v1c, 0, hp)
        o1 = jnp.concatenate([oa.reshape(RES, HALF, LANES), ob.reshape(RES, HALF, LANES)], axis=1)
        o1_ref[:, :, cols] = o1.astype(BF16)
        put(0, hp, sa)
        put(1, hp, sb)

        q2 = q2_ref[:, :, cols]
        k2p, k2c = k2p_ref[:, :, cols], k2c_ref[:, :, cols]
        v2p, v2c = v2p_ref[:, :, cols], v2c_ref[:, :, cols]
        q2a = q2[:, :QUARTER].reshape(SPAN, LANES)
        q2b = q2[:, QUARTER:].reshape(SPAN, LANES)
        k2a = jnp.concatenate([k2p, k2c[:, :QUARTER]], axis=1).reshape(2 * SPAN, LANES)
        v2a = jnp.concatenate([v2p, v2c[:, :QUARTER]], axis=1).reshape(2 * SPAN, LANES)
        oa, sa = unit(q2a, k2a, v2a, table_mid_a, hp)
        ob, sb = unit(q2b, k2c.reshape(2 * SPAN, LANES), v2c.reshape(2 * SPAN, LANES), 1, hp)
        o2_ref[:, :QUARTER, cols] = oa.reshape(SUB, QUARTER, LANES).astype(BF16)
        o2_ref[:, QUARTER:, cols] = ob.reshape(SUB, QUARTER, LANES).astype(BF16)
        put(2, hp, sa)
        put(3, hp, sb)

        q3, k3, v3 = q3_ref[:, cols], k3_ref[:, cols], v3_ref[:, cols]
        oa, sa = unit(q3[:SPAN], k3[:SPAN], v3[:SPAN], 2, hp, own_block_only=True)
        ob, sb = unit(q3[SPAN:], k3, v3, 2, hp)
        o3_ref[:SPAN, cols] = oa.astype(BF16)
        o3_ref[SPAN:, cols] = ob.astype(BF16)
        put(4, hp, sa)
        put(5, hp, sb)

    l1a, l1b, l2a, l2b, l3a, l3b = [m_scr[u] + jnp.log(l_scr[u]) for u in range(UNITS)]
    l1_ref[...] = jnp.concatenate(
        [l1a.reshape(RES, HALF, LANES), l1b.reshape(RES, HALF, LANES)], axis=1)
    l2_ref[:, :QUARTER] = l2a.reshape(SUB, QUARTER, LANES)
    l2_ref[:, QUARTER:] = l2b.reshape(SUB, QUARTER, LANES)
    l3_ref[:SPAN] = l3a
    l3_ref[SPAN:] = l3b


def _attention(kv_local, qkv):
    batch, seq, n_qkv = qkv.shape
    width = B_HEADS * B_HEAD_DIM
    per = seq // RES
    n_steps = seq // (2 * SPAN)
    by_res = qkv.reshape(batch, RES, per, n_qkv)
    by_sub = qkv.reshape(batch, SUB, MID, per, n_qkv)

    def col(g, part):
        return 0 if g == 0 else 3 * g - 2 + part

    mid_steps = n_steps // MID

    def mid_spec(rows, part, prev):
        def index_map(b, s):
            blk = s % mid_steps
            if prev:
                blk = jnp.maximum(2 * blk - 1, 0)
            return (b, 0, s // mid_steps, blk, col(1, part))
        return pl.BlockSpec((None, SUB, None, rows, width), index_map)

    in_specs = [
        pl.BlockSpec((None, RES, 2 * HALF, width), lambda b, s: (b, 0, s, col(0, 0))),
        pl.BlockSpec((None, SPAN, width), lambda b, s: (b, jnp.maximum(2 * s - 1, 0), 0)),
        pl.BlockSpec((None, 2 * SPAN, width), lambda b, s: (b, s, 0)),
        pl.BlockSpec((None, SPAN, width), lambda b, s: (b, jnp.maximum(2 * s - 1, 0), 1)),
        pl.BlockSpec((None, 2 * SPAN, width), lambda b, s: (b, s, 1)),
        mid_spec(2 * QUARTER, 0, False),
        mid_spec(QUARTER, 1, True), mid_spec(2 * QUARTER, 1, False),
        mid_spec(QUARTER, 2, True), mid_spec(2 * QUARTER, 2, False),
        pl.BlockSpec((None, None, per, width), lambda b, s: (b, s, 0, col(2, 0))),
        pl.BlockSpec((None, None, per, width), lambda b, s: (b, s, 0, col(2, 1))),
        pl.BlockSpec((None, None, per, width), lambda b, s: (b, s, 0, col(2, 2))),
    ]
    operands = [by_res] + [kv_local] * 4 + [by_sub] * 5 + [by_res] * 3

    out_specs, out_shapes = [], []

    def outs_for(w, dtype):
        return (
            [pl.BlockSpec((None, RES, 2 * HALF, w), lambda b, s: (b, 0, s, 0)),
             pl.BlockSpec((None, SUB, None, 2 * QUARTER, w),
                          lambda b, s: (b, 0, s // mid_steps, s % mid_steps, 0)),
             pl.BlockSpec((None, None, per, w), lambda b, s: (b, s, 0, 0))],
            [jax.ShapeDtypeStruct((batch, RES, per, w), dtype),
             jax.ShapeDtypeStruct((batch, SUB, MID, per, w), dtype),
             jax.ShapeDtypeStruct((batch, RES, per, w), dtype)])
    o_specs, o_shapes = outs_for(width, BF16)
    l_specs, l_shapes = outs_for(LANES, F32)
    for g in range(N_PAT):
        out_specs += [o_specs[g], l_specs[g]]
        out_shapes += [o_shapes[g], l_shapes[g]]

    outs = pl.pallas_call(
        _attn_kernel,
        grid=(batch, n_steps),
        in_specs=in_specs,
        out_specs=out_specs,
        out_shape=out_shapes,
        scratch_shapes=[pltpu.VMEM((N_PAT + 2, B_HEADS, SPAN, 2 * SPAN), F32),
                        pltpu.VMEM((UNITS, SPAN, LANES), F32),
                        pltpu.VMEM((UNITS, SPAN, LANES), F32)],
        compiler_params=pltpu.CompilerParams(
            dimension_semantics=("arbitrary", "arbitrary"),
            vmem_limit_bytes=VMEM_LIMIT_BYTES),
        name="dilated_attn",
    )(*operands)
    o = [outs[2 * g].reshape(batch, seq, width) for g in range(N_PAT)]
    lse = [outs[2 * g + 1].reshape(batch, seq, LANES) for g in range(N_PAT)]
    return o, lse


def _attn_out_kernel(o0_ref, o1_ref, o2_ref, l0_ref, l1_ref, l2_ref, x_ref, mod_ref, w_out_ref,
                     ln_g_ref, ln_b_ref, mlp_mod_ref, w_up_ref, w_down_ref, mlp_g_ref, mlp_b_ref,
                     out_ref):
    tm = out_ref.shape[0]
    read = lambda ref: ref[...].reshape(tm, ref.shape[-1])
    lses = [read(l0_ref), read(l1_ref), read(l2_ref)]
    mx = jnp.maximum(jnp.maximum(lses[0], lses[1]), lses[2])
    es = [jnp.exp(l - mx) for l in lses]
    den = es[0] + es[1] + es[2]
    src_lane = lax.broadcasted_iota(jnp.int32, (LANES, D_MODEL), 0)
    head_of_col = lax.broadcasted_iota(jnp.int32, (LANES, D_MODEL), 1) // B_HEAD_DIM
    expand = jnp.logical_and(src_lane < 2 * B_HEADS, src_lane % B_HEADS == head_of_col)
    expand = expand.astype(BF16)
    lane = lax.broadcasted_iota(jnp.int32, lses[0].shape, 1)
    o = read(o2_ref).astype(F32)
    base = o
    for e, o_ref in zip(es[:2], (o0_ref, o1_ref)):
        wts = e / den
        head = wts.astype(BF16).astype(F32)
        rest = pltpu.roll(wts - head, B_HEADS, axis=1)
        split = jnp.where(lane < B_HEADS, head, jnp.where(lane < 2 * B_HEADS, rest, 0.0))
        w = jnp.dot(split.astype(BF16), expand, preferred_element_type=F32)
        o = o + w * (read(o_ref).astype(F32) - base)
    y = jnp.dot(o.astype(BF16), w_out_ref[...], preferred_element_type=F32)
    x = _post_norm(read(x_ref), y, mod_ref[...], ln_g_ref[...], ln_b_ref[...])
    out = _mlp_rows(x, mlp_mod_ref[...], w_up_ref, w_down_ref, mlp_g_ref, mlp_b_ref)
    out_ref[...] = jnp.swapaxes(out.reshape(RES, tm // RES, D_MODEL), 0, 1).reshape(tm, D_MODEL)


def _attn_out_mlp(o, lse, x, mod, w_out, ln_g, ln_b, mlp_mod, w_up, w_down, mlp_g, mlp_b):
    batch, seq, _ = x.shape
    tm = 512
    row = lambda a: a.reshape(1, -1)
    by_res = lambda a: a.reshape(batch, RES, seq // RES, a.shape[-1])
    tile = lambda w: pl.BlockSpec((None, RES, tm // RES, w), lambda b, i: (b, 0, i, 0))
    per_batch = pl.BlockSpec((None, 3, D_MODEL), lambda b, i: (b, 0, 0))
    return pl.pallas_call(
        _attn_out_kernel,
        grid=(batch, seq // tm),
        in_specs=[tile(D_MODEL)] * 3 + [tile(LANES)] * 3 + [
            tile(D_MODEL),
            per_batch,
            _resident((D_MODEL, D_MODEL)),
            _resident((1, D_MODEL)),
            _resident((1, D_MODEL)),
            per_batch,
            _resident((D_MODEL, D_FF)),
            _resident((D_FF, D_MODEL)),
            _resident((1, D_MODEL)),
            _resident((1, D_MODEL)),
        ],
        out_specs=pl.BlockSpec((None, tm, D_MODEL), lambda b, i: (b, i, 0)),
        out_shape=jax.ShapeDtypeStruct(x.shape, F32),
        compiler_params=pltpu.CompilerParams(
            dimension_semantics=("arbitrary", "arbitrary"),
            vmem_limit_bytes=VMEM_LIMIT_BYTES),
        name="attn_out_mlp",
    )(*[by_res(a) for a in o], *[by_res(a) for a in lse], by_res(x), mod, w_out.astype(BF16),
      row(ln_g), row(ln_b), mlp_mod, w_up.astype(BF16), w_down.astype(BF16), row(mlp_g),
      row(mlp_b))


def kernel(x, c, ada_w, ada_b, ln_g, ln_b, a_w_in, a_b_in, a_vn_g, a_vn_b, a_w_s, a_b_s, a_w_out,
           b_w_qkv, b_w_out, mlp_w_up, mlp_w_down):
    mods = _ada_mod(c, ada_w, ada_b)
    x = _mixer_a_mlp(x, mods[0], a_w_in[0], a_b_in[0], a_vn_g[0], a_vn_b[0], a_w_s[0], a_b_s[0],
                     a_w_out[0], ln_g[0, 0], ln_b[0, 0],
                     mods[1], mlp_w_up[0], mlp_w_down[0], ln_g[0, 1], ln_b[0, 1])
    kv_local, qkv, x_res = _qkv(x, mods[2], b_w_qkv[0])
    o, lse = _attention(kv_local, qkv)
    return _attn_out_mlp(o, lse, x_res, mods[2], b_w_out[0], ln_g[1, 0], ln_b[1, 0],
                         mods[3], mlp_w_up[1], mlp_w_down[1], ln_g[1, 1], ln_b[1, 1])
```
